```python
import math
import jax, jax.numpy as jnp
from jax import lax
import numpy as np

D_MODEL = 1024
BATCH = 16
SEQ = 2048
DEPTH = 4

CTX_LEN = 256
GRID_W = 64
N_MOD = 6
EPS = 1e-6
ATTN_HEADS = 4
HEAD_DIM = 64
ATTN_VDIM = 2 * HEAD_DIM
Q_WIDTH = ATTN_HEADS * 2 * HEAD_DIM
ATTN_WIDTH = ATTN_HEADS * ATTN_VDIM
ROPE_BASE = 10000.0
Q_BLOCK = 128
SSD_HEADDIM = 64
SSD_GROUPS = 2
SSD_HPG = 4
SSD_HEADS = SSD_GROUPS * SSD_HPG
SSD_INNER = SSD_HEADS * SSD_HEADDIM
D_STATE = 128
D_CONV = 3
CHUNK = 128
CONV_DIM = SSD_INNER + 2 * SSD_GROUPS * D_STATE
MIX_WIDTH = ATTN_WIDTH + SSD_INNER
P_IN = 2 * Q_WIDTH + ATTN_WIDTH + SSD_INNER + CONV_DIM + 2 * SSD_HEADS
N_EXPERTS = 16
EC_CAPACITY = 2
D_EXPERT = 1408

kernel_name = "hybrid_diffattn_ssd_ecmoe_dit"


def rmsnorm(x, w):
    xf = x.astype(jnp.float32)
    y = xf * lax.rsqrt(jnp.mean(xf * xf, axis=-1, keepdims=True) + EPS)
    return (y * w.astype(jnp.float32)).astype(x.dtype)


def axial_rope_tables(n):
    rows = n // GRID_W
    row = jnp.repeat(jnp.arange(rows), GRID_W)
    col = jnp.tile(jnp.arange(GRID_W), rows)
    n_freq = HEAD_DIM // 4
    inv = ROPE_BASE ** (-jnp.arange(n_freq, dtype=jnp.float32) / n_freq)
    ang = jnp.stack([row, col], axis=-1).astype(jnp.float32)[..., None] * inv
    return jnp.cos(ang), jnp.sin(ang)


def apply_rope(x, cos, sin):
    xr = x.reshape(x.shape[:-1] + (2, 2, HEAD_DIM // 4))
    x1, x2 = xr[..., 0, :], xr[..., 1, :]
    c = cos[:, None, None].astype(x.dtype)
    s = sin[:, None, None].astype(x.dtype)
    out = jnp.stack([x1 * c - x2 * s, x1 * s + x2 * c], axis=-2)
    return out.reshape(x.shape)


def split_proj(p):
    idx = [int(v) for v in np.cumsum([Q_WIDTH, Q_WIDTH, ATTN_WIDTH, SSD_INNER, CONV_DIM, SSD_HEADS])]
    return jnp.split(p, idx, axis=-1)


def diff_attend(q, k_all, v_all, lam):
    s = jnp.einsum("bqhid,bkhid->bhiqk", q, k_all).astype(jnp.float32) * (HEAD_DIM ** -0.5)
    p = jax.nn.softmax(s, axis=-1)
    w = p[:, :, 0] - lam * p[:, :, 1]
    return jnp.einsum("bhqk,bkhe->bqhe", w.astype(v_all.dtype), v_all)


def centred_dwconv(x, w, bias):
    pad = D_CONV // 2
    l = x.shape[1]
    xp = jnp.pad(x, ((0, 0), (pad, pad), (0, 0)))
    out = bias
    for tap in range(D_CONV):
        out = out + xp[:, tap:tap + l] * w[tap]
    return out


def ssd_chunked(xs, dt, a, bm, cm, h0):
    f32 = jnp.float32
    b, l, g, r, p = xs.shape
    nst = bm.shape[-1]
    nc = l // CHUNK
    dtf = dt.astype(f32)
    xdt = (xs.astype(f32) * dtf[..., None]).reshape(b, nc, CHUNK, g, r, p)
    a_cs = jnp.cumsum((dtf * a).reshape(b, nc, CHUNK, g, r), axis=2)
    bc = bm.astype(f32).reshape(b, nc, CHUNK, g, nst)
    cc = cm.astype(f32).reshape(b, nc, CHUNK, g, nst)
    tri = jnp.tril(jnp.ones((CHUNK, CHUNK), bool))[:, :, None, None]
    seg = a_cs[:, :, :, None] - a_cs[:, :, None, :]
    decay = jnp.exp(jnp.where(tri, seg, -jnp.inf))
    cb = jnp.einsum("bclgn,bcsgn->bclsg", cc, bc)
    y_diag = jnp.einsum("bclsgr,bcsgrp->bclgrp", cb[..., None] * decay, xdt)
    to_end = jnp.exp(a_cs[:, :, -1:] - a_cs)
    states = jnp.einsum("bclgn,bclgrp->bcgrpn", bc, xdt * to_end[..., None])
    chunk_decay = jnp.exp(a_cs[:, :, -1])

    def step(h, inp):
        s, d = inp
        return h * d[..., None, None] + s, h

    h_final, h_in = lax.scan(step, h0.astype(f32),
                             (jnp.moveaxis(states, 1, 0), jnp.moveaxis(chunk_decay, 1, 0)))
    h_in = jnp.moveaxis(h_in, 0, 1)
    y_off = jnp.einsum("bclgn,bcgrpn->bclgrp", cc, h_in) * jnp.exp(a_cs)[..., None]
    y = (y_diag + y_off).reshape(b, l, g, r, p)
    return y.astype(xs.dtype), h_final


def ssd_bidirectional(xs, bm, cm, dt_f, dt_b, a_f, a_b, h0_f, h0_b):
    flip = lambda t: jnp.flip(t, axis=1)
    y_f, h_f = ssd_chunked(xs, dt_f, a_f, bm, cm, h0_f)
    y_b, h_b = ssd_chunked(flip(xs), flip(dt_b), a_b, flip(bm), flip(cm), h0_b)
    return y_f + flip(y_b), h_f, h_b


def ssd_inputs(xbc, dtf, dtb, conv_w, conv_b, dt_bias_f, dt_bias_b):
    b, l, _ = xbc.shape
    xbc = jax.nn.silu(centred_dwconv(xbc, conv_w, conv_b))
    xs, bm, cm = jnp.split(xbc, [SSD_INNER, SSD_INNER + SSD_GROUPS * D_STATE], axis=-1)
    xs = xs.reshape(b, l, SSD_GROUPS, SSD_HPG, SSD_HEADDIM)
    bm = bm.reshape(b, l, SSD_GROUPS, D_STATE)
    cm = cm.reshape(b, l, SSD_GROUPS, D_STATE)
    dt_f = jax.nn.softplus(dtf + dt_bias_f).reshape(b, l, SSD_GROUPS, SSD_HPG)
    dt_b = jax.nn.softplus(dtb + dt_bias_b).reshape(b, l, SSD_GROUPS, SSD_HPG)
    return xs, bm, cm, dt_f, dt_b


def gated_group_rmsnorm(y, z, w):
    b, l = z.shape[:2]
    g = (y.reshape(b, l, SSD_INNER) * jax.nn.silu(z)).reshape(b, l, SSD_GROUPS, SSD_INNER // SSD_GROUPS)
    gf = g.astype(jnp.float32)
    gn = gf * lax.rsqrt(jnp.mean(gf * gf, axis=-1, keepdims=True) + EPS)
    return (gn.reshape(b, l, SSD_INNER) * w.astype(jnp.float32)).astype(z.dtype)


def hybrid_mixer(h, hc, cos, sin, lam_init, with_ctx_out, w_in, q_norm_w, k_norm_w,
                 lq1, lk1, lq2, lk2, subln_w, conv_w, conv_b, dt_bias_f, dt_bias_b,
                 a_log_f, a_log_b, d_skip, ssd_norm_w, w_out):
    b, n, _ = h.shape
    m = hc.shape[1]
    q, k, v, z, xbc, dtf, dtb = split_proj(h @ w_in)
    qc, kc, vc, zc, xbcc, dtfc, dtbc = split_proj(hc @ w_in)

    heads = lambda t, w: rmsnorm(t.reshape(t.shape[0], t.shape[1], ATTN_HEADS, 2, HEAD_DIM), w)
    q = apply_rope(heads(q, q_norm_w), cos, sin)
    k = apply_rope(heads(k, k_norm_w), cos, sin)
    qc = heads(qc, q_norm_w)
    kc = heads(kc, k_norm_w)
    v = v.reshape(b, n, ATTN_HEADS, ATTN_VDIM)
    vc = vc.reshape(b, m, ATTN_HEADS, ATTN_VDIM)
    f32 = jnp.float32
    lam = (jnp.exp(jnp.sum(lq1.astype(f32) * lk1.astype(f32)))
           - jnp.exp(jnp.sum(lq2.astype(f32) * lk2.astype(f32))) + lam_init)
    k_all = jnp.concatenate([kc, k], axis=1)
    v_all = jnp.concatenate([vc, v], axis=1)
    nblk = n // Q_BLOCK
    qb = jnp.moveaxis(q.reshape(b, nblk, Q_BLOCK, ATTN_HEADS, 2, HEAD_DIM), 1, 0)
    ob = lax.map(lambda qq: diff_attend(qq, k_all, v_all, lam), qb)
    o = jnp.moveaxis(ob, 0, 1).reshape(b, n, ATTN_HEADS, ATTN_VDIM)
    attn = (rmsnorm(o, subln_w) * (1.0 - lam_init)).reshape(b, n, ATTN_WIDTH)

    a_f = -jnp.exp(a_log_f.astype(f32)).reshape(SSD_GROUPS, SSD_HPG)
    a_b = -jnp.exp(a_log_b.astype(f32)).reshape(SSD_GROUPS, SSD_HPG)
    d_res = d_skip.reshape(SSD_GROUPS, SSD_HPG, 1)
    xs_c, bm_c, cm_c, dtf_c, dtb_c = ssd_inputs(xbcc, dtfc, dtbc, conv_w, conv_b, dt_bias_f, dt_bias_b)
    h0 = jnp.zeros((b, SSD_GROUPS, SSD_HPG, SSD_HEADDIM, D_STATE), f32)
    y_c, hf_c, hb_c = ssd_bidirectional(xs_c, bm_c, cm_c, dtf_c, dtb_c, a_f, a_b, h0, h0)
    xs, bm, cm, dt_f, dt_b = ssd_inputs(xbc, dtf, dtb, conv_w, conv_b, dt_bias_f, dt_bias_b)
    y, _, _ = ssd_bidirectional(xs, bm, cm, dt_f, dt_b, a_f, a_b, hf_c, hb_c)
    ssd = gated_group_rmsnorm(y + d_res * xs, z, ssd_norm_w)

    mix_lat = jnp.concatenate([attn, ssd], axis=-1) @ w_out
    if not with_ctx_out:
        return mix_lat, None
    oc = diff_attend(qc, kc, vc, lam)
    attn_c = (rmsnorm(oc, subln_w) * (1.0 - lam_init)).reshape(b, m, ATTN_WIDTH)
    ssd_c = gated_group_rmsnorm(y_c + d_res * xs_c, zc, ssd_norm_w)
    mix_ctx = jnp.concatenate([attn_c, ssd_c], axis=-1) @ w_out
    return mix_lat, mix_ctx


def expert_choice_ffn(h, w_router, w_gate, w_up, w_down):
    b, n, d = h.shape
    cap = EC_CAPACITY * n // N_EXPERTS
    aff = jax.nn.softmax(jnp.einsum("bnd,de->ben", h, w_router).astype(jnp.float32), axis=1)
    gate, idx = lax.top_k(aff, cap)
    xe = jax.vmap(lambda hb, ib: hb[ib])(h, idx)
    hid = jax.nn.silu(jnp.einsum("becd,edf->becf", xe, w_gate)) * jnp.einsum("becd,edf->becf", xe, w_up)
    ye = jnp.einsum("becf,efd->becd", hid, w_down) * gate[..., None].astype(h.dtype)
    return jax.vmap(lambda ib, yb: jnp.zeros((n, d), yb.dtype).at[ib.reshape(-1)].add(yb.reshape(-1, d)))(idx, ye)


def setup_inputs(seed: int = 0) -> dict:
    key = jax.random.key(seed)
    ks = jax.random.split(key, 32)
    f32 = jnp.float32
    L = DEPTH
    nrm = lambda k, shape, scale: jax.random.normal(k, shape, f32) * scale
    gain = lambda k, shape: 1.0 + 0.02 * jax.random.normal(k, shape, f32)

    def dt_bias(k):
        dt0 = jnp.exp(jax.random.uniform(k, (L, SSD_HEADS), f32, math.log(1e-3), math.log(1e-1)))
        return dt0 + jnp.log(-jnp.expm1(-dt0))

    return {
        "x": nrm(ks[0], (BATCH, SEQ, D_MODEL), 1.0),
        "c": nrm(ks[1], (BATCH, D_MODEL), 1.0),
        "ctx": nrm(ks[2], (BATCH, CTX_LEN, D_MODEL), 1.0),
        "c_ctx": nrm(ks[3], (D_MODEL,), 1.0),
        "w_mod": nrm(ks[4], (L, D_MODEL, N_MOD * D_MODEL), 0.5 * D_MODEL ** -0.5),
        "b_mod": nrm(ks[5], (L, N_MOD * D_MODEL), 0.02),
        "norm1_w": gain(ks[6], (L, D_MODEL)),
        "norm2_w": gain(ks[7], (L, D_MODEL)),
        "w_in": nrm(ks[8], (L, D_MODEL, P_IN), D_MODEL ** -0.5),
        "q_norm_w": gain(ks[9], (L, HEAD_DIM)),
        "k_norm_w": gain(ks[10], (L, HEAD_DIM)),
        "lambda_q1": nrm(ks[11], (L, HEAD_DIM), 0.1),
        "lambda_k1": nrm(ks[12], (L, HEAD_DIM), 0.1),
        "lambda_q2": nrm(ks[13], (L, HEAD_DIM), 0.1),
        "lambda_k2": nrm(ks[14], (L, HEAD_DIM), 0.1),
        "subln_w": gain(ks[15], (L, ATTN_VDIM)),
        "conv_w": nrm(ks[16], (L, D_CONV, CONV_DIM), D_CONV ** -0.5),
        "conv_b": nrm(ks[17], (L, CONV_DIM), 0.02),
        "dt_bias_f": dt_bias(ks[18]),
        "dt_bias_b": dt_bias(ks[19]),
        "a_log_f": jnp.log(jax.random.uniform(ks[20], (L, SSD_HEADS), f32, 1.0, 16.0)),
        "a_log_b": jnp.log(jax.random.uniform(ks[21], (L, SSD_HEADS), f32, 1.0, 16.0)),
        "d_skip": gain(ks[22], (L, SSD_HEADS)),
        "ssd_norm_w": gain(ks[23], (L, SSD_INNER)),
        "w_out": nrm(ks[24], (L, MIX_WIDTH, D_MODEL), MIX_WIDTH ** -0.5),
        "w_router": nrm(ks[25], (L, D_MODEL, N_EXPERTS), D_MODEL ** -0.5),
        "w_gate": nrm(ks[26], (L, N_EXPERTS, D_MODEL, D_EXPERT), D_MODEL ** -0.5),
        "w_up": nrm(ks[27], (L, N_EXPERTS, D_MODEL, D_EXPERT), D_MODEL ** -0.5),
        "w_down": nrm(ks[28], (L, N_EXPERTS, D_EXPERT, D_MODEL), D_EXPERT ** -0.5),
    }


def reference(x, c, ctx, c_ctx, w_mod, b_mod, norm1_w, norm2_w, w_in, q_norm_w, k_norm_w,
              lambda_q1, lambda_k1, lambda_q2, lambda_k2, subln_w, conv_w, conv_b,
              dt_bias_f, dt_bias_b, a_log_f, a_log_b, d_skip, ssd_norm_w, w_out,
              w_router, w_gate, w_up, w_down):
    n = x.shape[1]
    cos, sin = axial_rope_tables(n)
    silu_c = jax.nn.silu(c)
    silu_cc = jax.nn.silu(c_ctx)
    x_lat, x_ctx = x, ctx
    for l in range(DEPTH):
        update_ctx = l < DEPTH - 1
        lam_init = 0.8 - 0.6 * math.exp(-0.3 * l)
        mod = silu_c @ w_mod[l] + b_mod[l]
        sh1, sc1, g1, sh2, sc2, g2 = jnp.split(mod[:, None, :], N_MOD, axis=-1)
        modc = silu_cc @ w_mod[l] + b_mod[l]
        sh1c, sc1c, g1c, sh2c, sc2c, g2c = jnp.split(modc, N_MOD, axis=-1)

        h = rmsnorm(x_lat, norm1_w[l]) * (1.0 + sc1) + sh1
        hc = rmsnorm(x_ctx, norm1_w[l]) * (1.0 + sc1c) + sh1c
        mix_lat, mix_ctx = hybrid_mixer(
            h, hc, cos, sin, lam_init, update_ctx, w_in[l], q_norm_w[l], k_norm_w[l],
            lambda_q1[l], lambda_k1[l], lambda_q2[l], lambda_k2[l], subln_w[l], conv_w[l], conv_b[l],
            dt_bias_f[l], dt_bias_b[l], a_log_f[l], a_log_b[l], d_skip[l], ssd_norm_w[l], w_out[l])
        x_lat = x_lat + g1 * mix_lat

        h2 = rmsnorm(x_lat, norm2_w[l]) * (1.0 + sc2) + sh2
        x_lat = x_lat + g2 * expert_choice_ffn(h2, w_router[l], w_gate[l], w_up[l], w_down[l])
        if update_ctx:
            x_ctx = x_ctx + g1c * mix_ctx
            hc2 = rmsnorm(x_ctx, norm2_w[l]) * (1.0 + sc2c) + sh2c
            x_ctx = x_ctx + g2c * expert_choice_ffn(hc2, w_router[l], w_gate[l], w_up[l], w_down[l])
    return x_lat
```

```python
import functools
import math

import jax
import jax.numpy as jnp
import numpy as np
from jax import lax
from jax.experimental import pallas as pl
from jax.experimental.pallas import tpu as pltpu

F32 = jnp.float32
BF16 = jnp.bfloat16
I32 = jnp.int32

D_MODEL = 1024
DEPTH = 4
GRID_W = 64
N_MOD = 6
EPS = 1e-6
ATTN_HEADS = 4
HEAD_DIM = 64
ATTN_VDIM = 128
Q_WIDTH = 512
ATTN_WIDTH = 512
ROPE_BASE = 10000.0
SSD_HEADDIM = 64
SSD_GROUPS = 2
SSD_HPG = 4
SSD_HEADS = 8
SSD_INNER = 512
D_STATE = 128
CHUNK = 128
CONV_DIM = 1024
P_MAIN = 3072
N_DT = 2 * SSD_HEADS
N_EXPERTS = 16
EC_CAPACITY = 2
D_EXPERT = 1408
LANES = 128
MOE_TOKENS = 2048
VMEM_LIMIT = 56 * 1024 * 1024


def _cparams(sem):
    return pltpu.CompilerParams(dimension_semantics=sem, vmem_limit_bytes=VMEM_LIMIT)


def _sigmoid(x):
    return 1.0 / (1.0 + jnp.exp(-x))


def _silu(x):
    return x * _sigmoid(x)


def _softplus(x):
    return jnp.maximum(x, 0.0) + jnp.log1p(jnp.exp(-jnp.abs(x)))


def _dot(a, b):
    return jnp.dot(a, b, preferred_element_type=F32)


def _dot_nt(a, b):
    return lax.dot_general(a, b, (((1,), (1,)), ((), ())), preferred_element_type=F32)


def _dot_tn(a, b):
    return lax.dot_general(a, b, (((0,), (0,)), ((), ())), preferred_element_type=F32)


def _split3(x):
    hi = x.astype(BF16)
    r = x - hi.astype(F32)
    mid = r.astype(BF16)
    lo = (r - mid.astype(F32)).astype(BF16)
    return hi, mid, lo


def _dot3_l(x, m):
    hi, mid, lo = _split3(x)
    return _dot(hi, m) + _dot(mid, m) + _dot(lo, m)


def _dot3_r(m, x):
    hi, mid, lo = _split3(x)
    return _dot(m, hi) + _dot(m, mid) + _dot(m, lo)


def _mod_kernel(cc_ref, w_ref, b_ref, o_ref):
    s = _silu(cc_ref[...]).astype(BF16)
    o_ref[0] = _dot(s, w_ref[0].astype(BF16)) + b_ref[0]


def _modulation(cc, w_mod, b_mod):
    nl, d, nm = w_mod.shape
    r = cc.shape[0]
    tn = 1536
    out = pl.pallas_call(
        _mod_kernel,
        grid=(nl, nm // tn),
        in_specs=[
            pl.BlockSpec((r, d), lambda l, j: (0, 0)),
            pl.BlockSpec((1, d, tn), lambda l, j: (l, 0, j)),
            pl.BlockSpec((1, 1, tn), lambda l, j: (l, 0, j)),
        ],
        out_specs=pl.BlockSpec((1, r, tn), lambda l, j: (l, 0, j)),
        out_shape=jax.ShapeDtypeStruct((nl, r, nm), F32),
        compiler_params=_cparams(("arbitrary", "arbitrary")),
        name="modulation",
    )(cc, w_mod, b_mod.reshape(nl, 1, nm))
    return out.reshape(nl, r, N_MOD, d)


def _rms_rows(x):
    return x * lax.rsqrt(jnp.mean(x * x, axis=-1, keepdims=True) + EPS)


def _head_norm(t, gmat, w128, rope_tabs, scale):
    ss = _dot((t * t).astype(BF16), gmat)
    y = t * lax.rsqrt(ss * (1.0 / HEAD_DIM) + EPS)
    outs = []
    for j in range(Q_WIDTH // LANES):
        yj = y[:, j * LANES:(j + 1) * LANES] * w128
        if rope_tabs is not None:
            cos, sa, sb = rope_tabs
            up = pltpu.roll(yj, LANES - 16, 1)
            dn = pltpu.roll(yj, 16, 1)
            yj = yj * cos + up * sa + dn * sb
        outs.append(yj * scale)
    return jnp.concatenate(outs, axis=1).astype(BF16)


def _inproj_kernel(*refs, rope):
    if rope:
        (x_ref, mod_ref, nw_ref, w_ref, wdt_ref, wdtt_ref, g_ref, qw_ref, kw_ref,
         cos_ref, sa_ref, sb_ref, q_o, k_o, v_o, z_o, xbc_o, dt_o, dtt_o) = refs
        tabs = (cos_ref[...], sa_ref[...], sb_ref[...])
    else:
        (x_ref, mod_ref, nw_ref, w_ref, wdt_ref, wdtt_ref, g_ref, qw_ref, kw_ref,
         q_o, k_o, v_o, z_o, xbc_o, dt_o, dtt_o) = refs
        tabs = None
    x = x_ref[0]
    sh = mod_ref[0, 0, 0:1, :]
    sc = mod_ref[0, 0, 1:2, :]
    h = (_rms_rows(x) * nw_ref[0] * (1.0 + sc) + sh).astype(BF16)
    p = _dot(h, w_ref[0])
    gmat = g_ref[...]
    q_o[0] = _head_norm(p[:, 0:512], gmat, qw_ref[0], tabs, HEAD_DIM ** -0.5)
    k_o[0] = _head_norm(p[:, 512:1024], gmat, kw_ref[0], tabs, 1.0)
    v_o[0] = p[:, 1024:1536].astype(BF16)
    z_o[0] = p[:, 1536:2048].astype(BF16)
    xbc_o[0] = p[:, 2048:3072].astype(BF16)
    dt_o[0] = _dot(h, wdt_ref[0])
    dtt_o[0] = _dot_nt(wdtt_ref[0], h)


def _in_projection(x, mod, mod_rows, layer, p, rope_tabs, tm):
    b, n, d = x.shape
    rope = rope_tabs is not None
    if mod_rows == 1:
        mod_map = lambda bi, i: (layer, mod.shape[1] - 1, 0, 0)
    else:
        mod_map = lambda bi, i: (layer, bi, 0, 0)
    lsel = lambda bi, i: (layer, 0, 0)
    in_specs = [
        pl.BlockSpec((1, tm, d), lambda bi, i: (bi, i, 0)),
        pl.BlockSpec((1, 1, N_MOD, d), mod_map),
        pl.BlockSpec((1, 1, d), lsel),
        pl.BlockSpec((1, d, P_MAIN), lsel),
        pl.BlockSpec((1, d, N_DT), lsel),
        pl.BlockSpec((1, N_DT, d), lsel),
        pl.BlockSpec((Q_WIDTH, Q_WIDTH), lambda bi, i: (0, 0)),
        pl.BlockSpec((1, 1, LANES), lsel),
        pl.BlockSpec((1, 1, LANES), lsel),
    ]
    args = [x, mod, p["norm1_w"], p["w_main"], p["w_dt"], p["w_dtt"], p["gmat"], p["q_norm_w"], p["k_norm_w"]]
    if rope:
        in_specs += [pl.BlockSpec((tm, LANES), lambda bi, i: (i, 0))] * 3
        args += list(rope_tabs)
    row = lambda w: pl.BlockSpec((1, tm, w), lambda bi, i: (bi, i, 0))
    out_specs = [row(512), row(512), row(512), row(512), row(1024), row(N_DT),
                 pl.BlockSpec((1, N_DT, tm), lambda bi, i: (bi, 0, i))]
    out_shape = [jax.ShapeDtypeStruct((b, n, 512), BF16)] * 4 + [
        jax.ShapeDtypeStruct((b, n, 1024), BF16),
        jax.ShapeDtypeStruct((b, n, N_DT), F32),
        jax.ShapeDtypeStruct((b, N_DT, n), F32)]
    return pl.pallas_call(
        functools.partial(_inproj_kernel, rope=rope),
        grid=(b, n // tm),
        in_specs=in_specs,
        out_specs=out_specs,
        out_shape=out_shape,
        compiler_params=_cparams(("parallel", "arbitrary")),
        name="in_projection_rope" if rope else "in_projection",
    )(*args)


def _attn_kernel(q_ref, k_ref, v_ref, lam_ref, sw_ref, o_ref, *, lam_init):
    lp = lam_ref[0]
    lam = (jnp.exp(jnp.sum(lp[0:1] * lp[1:2], axis=-1, keepdims=True))
           - jnp.exp(jnp.sum(lp[2:3] * lp[3:4], axis=-1, keepdims=True)) + lam_init)
    lane = lax.broadcasted_iota(I32, (1, LANES), 1)
    m_lo = jnp.where(lane < HEAD_DIM, 1.0, 0.0).astype(BF16)
    m_hi = jnp.where(lane < HEAD_DIM, 0.0, 1.0).astype(BF16)
    sw = sw_ref[0]
    for h in range(ATTN_HEADS):
        cols = slice(h * LANES, (h + 1) * LANES)
        qh = q_ref[0, :, cols]
        kh = k_ref[0, :, cols]
        vh = v_ref[0, :, cols]
        s1 = _dot_nt(qh * m_lo, kh)
        s2 = _dot_nt(qh * m_hi, kh)
        p1 = jnp.exp(s1 - jnp.max(s1, axis=-1, keepdims=True))
        p2 = jnp.exp(s2 - jnp.max(s2, axis=-1, keepdims=True))
        r1 = 1.0 / jnp.sum(p1, axis=-1, keepdims=True)
        r2 = lam / jnp.sum(p2, axis=-1, keepdims=True)
        w = (p1 * r1 - p2 * r2).astype(BF16)
        o = _dot(w, vh)
        o = _rms_rows(o) * sw * (1.0 - lam_init)
        o_ref[0, :, cols] = o.astype(BF16)


def _diff_attention(q, k_all, v_all, lam_p, subln_w, layer, lam_init, tq):
    b, n, _ = q.shape
    nk = k_all.shape[1]
    return pl.pallas_call(
        functools.partial(_attn_kernel, lam_init=lam_init),
        grid=(b, n // tq),
        in_specs=[
            pl.BlockSpec((1, tq, Q_WIDTH), lambda bi, i: (bi, i, 0)),
            pl.BlockSpec((1, nk, Q_WIDTH), lambda bi, i: (bi, 0, 0)),
            pl.BlockSpec((1, nk, ATTN_WIDTH), lambda bi, i: (bi, 0, 0)),
            pl.BlockSpec((1, 4, HEAD_DIM), lambda bi, i: (layer, 0, 0)),
            pl.BlockSpec((1, 1, ATTN_VDIM), lambda bi, i: (layer, 0, 0)),
        ],
        out_specs=pl.BlockSpec((1, tq, ATTN_WIDTH), lambda bi, i: (bi, i, 0)),
        out_shape=jax.ShapeDtypeStruct((b, n, ATTN_WIDTH), BF16),
        compiler_params=_cparams(("parallel", "arbitrary")),
        name="diff_attention",
    )(q, k_all, v_all, lam_p, subln_w)


def _ssd_kernel(xbc_ref, z_ref, dt_ref, dtt_ref, cw_ref, cb_ref, dtb_ref, dtbt_ref, al_ref, alt_ref,
                dsk_ref, nw_ref, ex_ref, h0_ref, y_ref, hfin_ref, xs_s, c_s, bt_s, yacc_s, ht_s, *, n):
    nc = n // CHUNK
    row = lax.broadcasted_iota(I32, (CHUNK, CHUNK), 0)
    col = lax.broadcasted_iota(I32, (CHUNK, CHUNK), 1)
    tri_lo = col <= row
    tri_up = col >= row
    t_lo = jnp.where(tri_lo, 1.0, 0.0).astype(BF16)
    t_up = jnp.where(tri_up, 1.0, 0.0).astype(BF16)
    a_row = -jnp.exp(al_ref[0])
    a_col = -jnp.exp(alt_ref[0])
    lane512 = lax.broadcasted_iota(I32, (1, SSD_INNER // 2), 1)
    head_masks = [jnp.where((lane512 // SSD_HEADDIM) == r, 1.0, 0.0).astype(BF16) for r in range(SSD_HPG)]

    def chunk_scan(c, d, xs, cm, bt):
        r0 = pl.multiple_of(c * CHUNK, CHUNK)
        dt_c = _softplus(dt_ref[0, pl.ds(r0, CHUNK), :] + dtb_ref[0])
        da_c = dt_c * a_row
        dt_r = _softplus(dtt_ref[0, :, pl.ds(r0, CHUNK)] + dtbt_ref[0])
        da_r = dt_r * a_col
        if d == 0:
            cs_c = _dot3_r(t_lo, da_c)
            cs_r = _dot3_l(da_r, t_up)
            edge = CHUNK - 1
            mask = tri_lo
        else:
            cs_c = _dot3_r(t_up, da_c)
            cs_r = _dot3_l(da_r, t_lo)
            edge = 0
            mask = tri_up
        ex = ex_ref[:, d * SSD_INNER:(d + 1) * SSD_INNER]
        dt_x = _dot3_l(dt_c, ex)
        cs_x = _dot3_l(cs_c, ex)
        cs_edge = cs_x[edge:edge + 1, :]
        xdt = xs * dt_x
        xdt_b = xdt.astype(BF16)
        xw = (xdt * jnp.exp(cs_edge - cs_x)).astype(BF16)
        ecs = jnp.exp(cs_x)
        cdec = jnp.exp(cs_edge)
        ys = []
        for g in range(SSD_GROUPS):
            gl = slice(g * 256, (g + 1) * 256)
            cg = cm[:, g * D_STATE:(g + 1) * D_STATE]
            btg = bt[g * D_STATE:(g + 1) * D_STATE, :]
            cbg = _dot(cg, btg)
            htg = ht_s[d, g]
            y_off = _dot(cg, htg.astype(BF16)) * ecs[:, gl]
            lms, rhs = [], []
            for r in range(SSD_HPG):
                hd = d * SSD_HEADS + g * SSD_HPG + r
                seg = cs_c[:, hd:hd + 1] - cs_r[hd:hd + 1, :]
                dec = jnp.exp(jnp.where(mask, seg, -jnp.inf))
                lms.append((cbg * dec).astype(BF16))
                rhs.append(xdt_b[:, gl] * head_masks[r])
            y_diag = _dot(jnp.concatenate(lms, axis=1), jnp.concatenate(rhs, axis=0))
            ys.append(y_diag + y_off)
            st = _dot(btg, xw[:, gl])
            ht_s[d, g] = htg * cdec[:, gl] + st
        return jnp.concatenate(ys, axis=1)

    ht_s[...] = h0_ref[0]
    cw = cw_ref[0]
    rows1 = lax.broadcasted_iota(I32, (CHUNK, 1), 0)

    def fwd_body(c, carry):
        r0 = pl.multiple_of(c * CHUNK, CHUNK)
        x = xbc_ref[0, pl.ds(r0, CHUNK), :].astype(F32)
        rp0 = pl.multiple_of(jnp.maximum(r0 - 16, 0), 16)
        rn0 = pl.multiple_of(jnp.minimum(r0 + CHUNK, n - 16), 16)
        prev = xbc_ref[0, pl.ds(rp0, 16), :].astype(F32)[15:16, :]
        nxt = xbc_ref[0, pl.ds(rn0, 16), :].astype(F32)[0:1, :]
        prev = jnp.where(c > 0, prev, 0.0)
        nxt = jnp.where(c < nc - 1, nxt, 0.0)
        x_prev = jnp.where(rows1 == 0, prev, pltpu.roll(x, 1, 0))
        x_next = jnp.where(rows1 == CHUNK - 1, nxt, pltpu.roll(x, CHUNK - 1, 0))
        xc = _silu(cb_ref[0] + x_prev * cw[0:1] + x * cw[1:2] + x_next * cw[2:3])
        xs = xc[:, 0:SSD_INNER]
        cm = xc[:, SSD_INNER + 256:SSD_INNER + 512].astype(BF16)
        bm = xc[:, SSD_INNER:SSD_INNER + 256]
        bt = jnp.concatenate([bm[:, 0:D_STATE].T, bm[:, D_STATE:2 * D_STATE].T], axis=0).astype(BF16)
        xs_s[pl.ds(r0, CHUNK), :] = xs
        c_s[pl.ds(r0, CHUNK), :] = cm
        bt_s[:, pl.ds(r0, CHUNK)] = bt
        yacc_s[pl.ds(r0, CHUNK), :] = chunk_scan(c, 0, xs, cm, bt)
        return carry

    lax.fori_loop(0, nc, fwd_body, 0)

    def bwd_body(i, carry):
        c = nc - 1 - i
        r0 = pl.multiple_of(c * CHUNK, CHUNK)
        xs = xs_s[pl.ds(r0, CHUNK), :]
        cm = c_s[pl.ds(r0, CHUNK), :]
        bt = bt_s[:, pl.ds(r0, CHUNK)]
        y = yacc_s[pl.ds(r0, CHUNK), :] + chunk_scan(c, 1, xs, cm, bt) + dsk_ref[0] * xs
        zz = z_ref[0, pl.ds(r0, CHUNK), :].astype(F32)
        g = y * _silu(zz)
        outs = []
        for gi in range(SSD_GROUPS):
            outs.append(_rms_rows(g[:, gi * 256:(gi + 1) * 256]))
        y_ref[0, pl.ds(r0, CHUNK), :] = (jnp.concatenate(outs, axis=1) * nw_ref[0]).astype(BF16)
        return carry

    lax.fori_loop(0, nc, bwd_body, 0)
    hfin_ref[0] = ht_s[...]


def _ssd_mixer(xbc, z, dt, dtt, h0, p, layer):
    b, n, _ = xbc.shape
    lsel3 = lambda bi: (layer, 0, 0)
    st_spec = pl.BlockSpec((1, 2, SSD_GROUPS, D_STATE, 256), lambda bi: (bi, 0, 0, 0, 0))
    return pl.pallas_call(
        functools.partial(_ssd_kernel, n=n),
        grid=(b,),
        in_specs=[
            pl.BlockSpec((1, n, CONV_DIM), lambda bi: (bi, 0, 0)),
            pl.BlockSpec((1, n, SSD_INNER), lambda bi: (bi, 0, 0)),
            pl.BlockSpec((1, n, N_DT), lambda bi: (bi, 0, 0)),
            pl.BlockSpec((1, N_DT, n), lambda bi: (bi, 0, 0)),
            pl.BlockSpec((1, 3, CONV_DIM), lsel3),
            pl.BlockSpec((1, 1, CONV_DIM), lsel3),
            pl.BlockSpec((1, 1, N_DT), lsel3),
            pl.BlockSpec((1, N_DT, 1), lsel3),
            pl.BlockSpec((1, 1, N_DT), lsel3),
            pl.BlockSpec((1, N_DT, 1), lsel3),
            pl.BlockSpec((1, 1, SSD_INNER), lsel3),
            pl.BlockSpec((1, 1, SSD_INNER), lsel3),
            pl.BlockSpec((N_DT, 2 * SSD_INNER), lambda bi: (0, 0)),
            st_spec,
        ],
        out_specs=[pl.BlockSpec((1, n, SSD_INNER), lambda bi: (bi, 0, 0)), st_spec],
        out_shape=[jax.ShapeDtypeStruct((b, n, SSD_INNER), BF16),
                   jax.ShapeDtypeStruct((b, 2, SSD_GROUPS, D_STATE, 256), F32)],
        scratch_shapes=[
            pltpu.VMEM((n, SSD_INNER), F32),
            pltpu.VMEM((n, 2 * D_STATE), BF16),
            pltpu.VMEM((2 * D_STATE, n), BF16),
            pltpu.VMEM((n, SSD_INNER), F32),
            pltpu.VMEM((2, SSD_GROUPS, D_STATE, 256), F32),
        ],
        compiler_params=_cparams(("parallel",)),
        name="ssd_mixer",
    )(xbc, z, dt, dtt, p["conv_w"], p["conv_b"], p["dt_bias"], p["dt_bias_t"], p["a_log"], p["a_log_t"],
      p["d_skip_x"], p["ssd_norm_w"], p["expand"], h0)


def _outproj_kernel(a_ref, s_ref, x_ref, mod_ref, wa_ref, ws_ref, nw_ref, wr_ref, xo_ref, h2_ref, aff_ref):
    mix = _dot(a_ref[0], wa_ref[0]) + _dot(s_ref[0], ws_ref[0])
    x = x_ref[0] + mod_ref[0, 0, 2:3, :] * mix
    xo_ref[0] = x
    h2 = (_rms_rows(x) * nw_ref[0] * (1.0 + mod_ref[0, 0, 4:5, :]) + mod_ref[0, 0, 3:4, :]).astype(BF16)
    h2_ref[0] = h2
    lg = _dot(h2, wr_ref[0])
    lgt = lg.T[0:N_EXPERTS, :]
    e = jnp.exp(lgt - jnp.max(lgt, axis=0, keepdims=True))
    aff_ref[0] = e / jnp.sum(e, axis=0, keepdims=True)


def _out_projection(attn, ssd, x, mod, mod_rows, layer, p, tm, group):
    b, n, d = x.shape
    nt = n // tm
    if mod_rows == 1:
        mod_map = lambda bi, i: (layer, mod.shape[1] - 1, 0, 0)
    else:
        mod_map = lambda bi, i: (layer, bi, 0, 0)
    lsel = lambda bi, i: (layer, 0, 0)
    row = lambda w: pl.BlockSpec((1, tm, w), lambda bi, i: (bi, i, 0))
    return pl.pallas_call(
        _outproj_kernel,
        grid=(b, nt),
        in_specs=[row(ATTN_WIDTH), row(SSD_INNER), row(d),
                  pl.BlockSpec((1, 1, N_MOD, d), mod_map),
                  pl.BlockSpec((1, ATTN_WIDTH, d), lsel),
                  pl.BlockSpec((1, SSD_INNER, d), lsel),
                  pl.BlockSpec((1, 1, d), lsel),
                  pl.BlockSpec((1, d, LANES), lsel)],
        out_specs=[row(d), row(d),
                   pl.BlockSpec((1, N_EXPERTS, tm), lambda bi, i: (bi // group, 0, (bi % group) * nt + i))],
        out_shape=[jax.ShapeDtypeStruct((b, n, d), F32),
                   jax.ShapeDtypeStruct((b, n, d), BF16),
                   jax.ShapeDtypeStruct((b // group, N_EXPERTS, group * n), F32)],
        compiler_params=_cparams(("parallel", "arbitrary")),
        name="out_projection",
    )(attn, ssd, x, mod, p["w_out_a"], p["w_out_s"], p["norm2_w"], p["w_router"])


def _route_kernel(aff_ref, slot_ref, *, n, group, cap):
    bits_all = pltpu.bitcast(aff_ref[0], I32)
    li = lax.broadcasted_iota(I32, (LANES, LANES), 0)
    lj = lax.broadcasted_iota(I32, (LANES, LANES), 1)
    strict_up = jnp.where(li < lj, 1.0, 0.0).astype(BF16)
    fcap = float(cap)

    def count(m):
        return jnp.sum(jnp.where(m, 1.0, 0.0), axis=-1, keepdims=True)

    def excl_prefix(m):
        total = jnp.zeros((N_EXPERTS, 1), F32)
        outs = []
        for j in range(n // LANES):
            blk = jnp.where(m[:, j * LANES:(j + 1) * LANES], 1.0, 0.0)
            outs.append(_dot(blk.astype(BF16), strict_up) + total)
            total = total + jnp.sum(blk, axis=-1, keepdims=True)
        return jnp.concatenate(outs, axis=1)

    for s in range(group):
        bits = bits_all[:, s * n:(s + 1) * n]

        def bisect(i, lo):
            cand = lo | (jnp.int32(1) << (30 - i))
            return jnp.where(count(bits >= cand) >= fcap, cand, lo)

        thr = lax.fori_loop(0, 31, bisect, jnp.zeros((N_EXPERTS, 1), I32))
        gt = bits > thr
        eq = bits == thr
        need = fcap - count(gt)
        sel = gt | (eq & (excl_prefix(eq) < need))
        pos = excl_prefix(sel)
        slot_ref[0, :, s * n:(s + 1) * n] = jnp.where(sel, pos.astype(I32) + s * cap, -1)


def _route(aff, n, group, cap):
    ng, _, t = aff.shape
    return pl.pallas_call(
        functools.partial(_route_kernel, n=n, group=group, cap=cap),
        grid=(ng,),
        in_specs=[pl.BlockSpec((1, N_EXPERTS, t), lambda g: (g, 0, 0))],
        out_specs=pl.BlockSpec((1, N_EXPERTS, t), lambda g: (g, 0, 0)),
        out_shape=jax.ShapeDtypeStruct((ng, N_EXPERTS, t), I32),
        compiler_params=_cparams(("parallel",)),
        name="ec_route",
    )(aff)


def _moe_kernel(slot_ref, aff_ref, h_ref, wg_ref, wu_ref, wd_ref, o_ref, *, slots, tile):
    e = pl.program_id(1)
    t = h_ref.shape[1]
    sl = slot_ref[0, 0]
    hit = lax.broadcasted_iota(I32, (slots, t), 0) == sl
    pmat = jnp.where(hit, 1.0, 0.0).astype(BF16)
    gate = jnp.sum(jnp.where(hit, aff_ref[0, 0], 0.0), axis=-1, keepdims=True)
    xe = _dot(pmat, h_ref[0]).astype(BF16)
    hid = (_silu(_dot(xe, wg_ref[0, 0])) * _dot(xe, wu_ref[0, 0])).astype(BF16)
    ye = (_dot(hid, wd_ref[0, 0]) * gate).astype(BF16)
    for j in range(t // tile):
        rows = slice(j * tile, (j + 1) * tile)
        contrib = _dot_tn(pmat[:, rows], ye)

        @pl.when(e == 0)
        def _():
            o_ref[0, rows, :] = contrib

        @pl.when(e > 0)
        def _():
            o_ref[0, rows, :] += contrib


def _moe(slot, aff, h2g, p, layer, slots):
    ng, t, d = h2g.shape
    sel = lambda g, e: (g, e, 0, 0)
    wsel = lambda g, e: (layer, e, 0, 0)
    return pl.pallas_call(
        functools.partial(_moe_kernel, slots=slots, tile=min(t, 512)),
        grid=(ng, N_EXPERTS),
        in_specs=[
            pl.BlockSpec((1, 1, 1, t), sel),
            pl.BlockSpec((1, 1, 1, t), sel),
            pl.BlockSpec((1, t, d), lambda g, e: (g, 0, 0)),
            pl.BlockSpec((1, 1, d, D_EXPERT), wsel),
            pl.BlockSpec((1, 1, d, D_EXPERT), wsel),
            pl.BlockSpec((1, 1, D_EXPERT, d), wsel),
        ],
        out_specs=pl.BlockSpec((1, t, d), lambda g, e: (g, 0, 0)),
        out_shape=jax.ShapeDtypeStruct((ng, t, d), F32),
        compiler_params=_cparams(("parallel", "arbitrary")),
        name="ec_moe",
    )(slot.reshape(ng, N_EXPERTS, 1, t), aff.reshape(ng, N_EXPERTS, 1, t), h2g,
      p["w_gate"], p["w_up"], p["w_down"])


def _residual_kernel(x_ref, y_ref, mod_ref, o_ref):
    o_ref[0] = x_ref[0] + mod_ref[0, 0, 5:6, :] * y_ref[0]


def _gated_residual(x, y, mod, mod_rows, layer, tm):
    b, n, d = x.shape
    if mod_rows == 1:
        mod_map = lambda bi, i: (layer, mod.shape[1] - 1, 0, 0)
    else:
        mod_map = lambda bi, i: (layer, bi, 0, 0)
    row = pl.BlockSpec((1, tm, d), lambda bi, i: (bi, i, 0))
    return pl.pallas_call(
        _residual_kernel,
        grid=(b, n // tm),
        in_specs=[row, row, pl.BlockSpec((1, 1, N_MOD, d), mod_map)],
        out_specs=row,
        out_shape=jax.ShapeDtypeStruct((b, n, d), F32),
        compiler_params=_cparams(("parallel", "arbitrary")),
        name="gated_residual",
    )(x, y, mod)


def _rope_tables(n):
    rows = n // GRID_W
    pos = np.stack([np.repeat(np.arange(rows), GRID_W), np.tile(np.arange(GRID_W), rows)], axis=-1)
    n_freq = HEAD_DIM // 4
    inv = jnp.asarray(ROPE_BASE, F32) ** (-jnp.arange(n_freq, dtype=F32) / n_freq)
    ang = jnp.asarray(pos, F32)[..., None] * inv
    cos, sin = jnp.cos(ang), jnp.sin(ang)
    zero = jnp.zeros_like(sin)
    cos64 = jnp.concatenate([cos, cos], axis=-1).reshape(n, HEAD_DIM)
    sa64 = jnp.concatenate([-sin, zero], axis=-1).reshape(n, HEAD_DIM)
    sb64 = jnp.concatenate([zero, sin], axis=-1).reshape(n, HEAD_DIM)
    two = lambda t: jnp.concatenate([t, t], axis=-1)
    return two(cos64), two(sa64), two(sb64)


def _prepare(w_in, norm1_w, norm2_w, q_norm_w, k_norm_w, lambda_q1, lambda_k1, lambda_q2, lambda_k2, subln_w,
             conv_w, conv_b, dt_bias_f, dt_bias_b, a_log_f, a_log_b, d_skip, ssd_norm_w, w_out, w_router,
             w_gate, w_up, w_down):
    nl = w_in.shape[0]
    gi = np.arange(Q_WIDTH) // HEAD_DIM
    lane_head = np.concatenate([np.arange(SSD_INNER) // SSD_HEADDIM, SSD_HEADS + np.arange(SSD_INNER) // SSD_HEADDIM])
    dt_bias = jnp.concatenate([dt_bias_f, dt_bias_b], axis=-1)
    a_log = jnp.concatenate([a_log_f, a_log_b], axis=-1)
    tile2 = lambda w: jnp.concatenate([w, w], axis=-1).reshape(nl, 1, LANES)
    return {
        "w_main": w_in[:, :, :P_MAIN].astype(BF16),
        "w_dt": w_in[:, :, P_MAIN:].astype(BF16),
        "w_dtt": jnp.swapaxes(w_in[:, :, P_MAIN:], 1, 2).astype(BF16),
        "gmat": jnp.asarray(gi[:, None] == gi[None, :], BF16),
        "norm1_w": norm1_w.reshape(nl, 1, D_MODEL),
        "norm2_w": norm2_w.reshape(nl, 1, D_MODEL),
        "q_norm_w": tile2(q_norm_w),
        "k_norm_w": tile2(k_norm_w),
        "lam": jnp.stack([lambda_q1, lambda_k1, lambda_q2, lambda_k2], axis=1),
        "subln_w": subln_w.reshape(nl, 1, ATTN_VDIM),
        "conv_w": conv_w,
        "conv_b": conv_b.reshape(nl, 1, CONV_DIM),
        "dt_bias": dt_bias.reshape(nl, 1, N_DT),
        "dt_bias_t": dt_bias.reshape(nl, N_DT, 1),
        "a_log": a_log.reshape(nl, 1, N_DT),
        "a_log_t": a_log.reshape(nl, N_DT, 1),
        "d_skip_x": jnp.repeat(d_skip, SSD_HEADDIM, axis=-1).reshape(nl, 1, SSD_INNER),
        "ssd_norm_w": ssd_norm_w.reshape(nl, 1, SSD_INNER),
        "expand": jnp.asarray(np.arange(N_DT)[:, None] == lane_head[None, :], BF16),
        "w_out_a": w_out[:, :ATTN_WIDTH, :].astype(BF16),
        "w_out_s": w_out[:, ATTN_WIDTH:, :].astype(BF16),
        "w_router": jnp.pad(w_router, ((0, 0), (0, 0), (0, LANES - N_EXPERTS))).astype(BF16),
        "w_gate": w_gate.astype(BF16),
        "w_up": w_up.astype(BF16),
        "w_down": w_down.astype(BF16),
    }


def _ffn(h2, aff, x_mid, mod, mod_rows, layer, p, group, tm):
    b, n, d = h2.shape
    cap = EC_CAPACITY * n // N_EXPERTS
    slot = _route(aff, n, group, cap)
    y = _moe(slot, aff, h2.reshape(b // group, group * n, d), p, layer, group * cap)
    return _gated_residual(x_mid, y.reshape(b, n, d), mod, mod_rows, layer, tm)


def kernel(x, c, ctx, c_ctx, w_mod, b_mod, norm1_w, norm2_w, w_in, q_norm_w, k_norm_w, lambda_q1, lambda_k1,
           lambda_q2, lambda_k2, subln_w, conv_w, conv_b, dt_bias_f, dt_bias_b, a_log_f, a_log_b, d_skip,
           ssd_norm_w, w_out, w_router, w_gate, w_up, w_down):
    b, n, d = x.shape
    m = ctx.shape[1]
    nl = w_in.shape[0]
    assert d == D_MODEL and n % GRID_W == 0 and n % CHUNK == 0 and m % CHUNK == 0
    assert MOE_TOKENS % n == 0 and MOE_TOKENS % m == 0
    g_lat, g_ctx = MOE_TOKENS // n, MOE_TOKENS // m
    assert b % g_lat == 0 and b % g_ctx == 0
    tm_lat, tm_ctx = min(n, 512), min(m, 256)
    tq = min(n, 256)

    p = _prepare(w_in, norm1_w, norm2_w, q_norm_w, k_norm_w, lambda_q1, lambda_k1, lambda_q2, lambda_k2,
                 subln_w, conv_w, conv_b, dt_bias_f, dt_bias_b, a_log_f, a_log_b, d_skip, ssd_norm_w, w_out,
                 w_router, w_gate, w_up, w_down)
    rope = _rope_tables(n)
    mod = _modulation(jnp.concatenate([c, c_ctx[None, :]], axis=0), w_mod, b_mod)
    h_zero = jnp.zeros((b, 2, SSD_GROUPS, D_STATE, 256), F32)

    x_lat, x_ctx = x, ctx
    for l in range(nl):
        update_ctx = l < nl - 1
        lam_init = 0.8 - 0.6 * math.exp(-0.3 * l)
        qc, kc, vc, zc, xbcc, dtc, dttc = _in_projection(x_ctx, mod, 1, l, p, None, tm_ctx)
        q, k, v, z, xbc, dt, dtt = _in_projection(x_lat, mod, b, l, p, rope, tm_lat)
        ssd_c, h_ctx = _ssd_mixer(xbcc, zc, dtc, dttc, h_zero, p, l)
        ssd, _ = _ssd_mixer(xbc, z, dt, dtt, h_ctx, p, l)
        k_all = jnp.concatenate([kc, k], axis=1)
        v_all = jnp.concatenate([vc, v], axis=1)
        attn = _diff_attention(q, k_all, v_all, p["lam"], p["subln_w"], l, lam_init, tq)
        x_mid, h2, aff = _out_projection(attn, ssd, x_lat, mod, b, l, p, tm_lat, g_lat)
        x_lat = _ffn(h2, aff, x_mid, mod, b, l, p, g_lat, tm_lat)
        if update_ctx:
            attn_c = _diff_attention(qc, kc, vc, p["lam"], p["subln_w"], l, lam_init, min(m, 256))
            xc_mid, hc2, aff_c = _out_projection(attn_c, ssd_c, x_ctx, mod, 1, l, p, tm_ctx, g_ctx)
            x_ctx = _ffn(hc2, aff_c, xc_mid, mod, 1, l, p, g_ctx, tm_ctx)
    return x_lat
```

```python
import functools
import math

import jax
import jax.numpy as jnp
import numpy as np
from jax import lax
from jax.experimental import pallas as pl
from jax.experimental.pallas import tpu as pltpu

F32 = jnp.float32
BF16 = jnp.bfloat16
I32 = jnp.int32

D_MODEL = 1024
DEPTH = 4
GRID_W = 64
N_MOD = 6
EPS = 1e-6
ATTN_HEADS = 4
HEAD_DIM = 64
ATTN_VDIM = 128
Q_WIDTH = 512
ATTN_WIDTH = 512
ROPE_BASE = 10000.0
SSD_HEADDIM = 64
SSD_GROUPS = 2
SSD_HPG = 4
SSD_HEADS = 8
SSD_INNER = 512
D_STATE = 128
CHUNK = 128
CONV_DIM = 1024
P_MAIN = 3072
N_DT = 2 * SSD_HEADS
N_EXPERTS = 16
EC_CAPACITY = 2
D_EXPERT = 1408
LANES = 128
MOE_TOKENS = 2048
QK_SCALE = HEAD_DIM ** -0.5 * math.log2(math.e)
LOGIT_MARGIN = 1.03
ATTN_SAFE_LOGIT = 30.0
VMEM_LIMIT = 56 * 1024 * 1024


def _cparams(sem):
    return pltpu.CompilerParams(dimension_semantics=sem, vmem_limit_bytes=VMEM_LIMIT)


def _sigmoid(x):
    return 1.0 / (1.0 + jnp.exp(-x))


def _silu(x):
    return x * _sigmoid(x)


def _softplus(x):
    return jnp.maximum(x, 0.0) + jnp.log1p(jnp.exp(-jnp.abs(x)))


def _dot(a, b):
    return jnp.dot(a, b, preferred_element_type=F32)


def _dot_nt(a, b):
    return lax.dot_general(a, b, (((1,), (1,)), ((), ())), preferred_element_type=F32)


def _dot_tn(a, b):
    return lax.dot_general(a, b, (((0,), (0,)), ((), ())), preferred_element_type=F32)


def _split3(x):
    hi = x.astype(BF16)
    r = x - hi.astype(F32)
    mid = r.astype(BF16)
    lo = (r - mid.astype(F32)).astype(BF16)
    return hi, mid, lo


def _dot3_l(x, m):
    hi, mid, lo = _split3(x)
    return _dot(hi, m) + _dot(mid, m) + _dot(lo, m)


def _dot3_r(m, x):
    hi, mid, lo = _split3(x)
    return _dot(m, hi) + _dot(m, mid) + _dot(m, lo)


def _mod_kernel(cc_ref, w_ref, b_ref, o_ref):
    s = _silu(cc_ref[...]).astype(BF16)
    o_ref[0] = _dot(s, w_ref[0].astype(BF16)) + b_ref[0]


def _modulation(cc, w_mod, b_mod):
    nl, d, nm = w_mod.shape
    r = cc.shape[0]
    tn = 1536
    out = pl.pallas_call(
        _mod_kernel,
        grid=(nl, nm // tn),
        in_specs=[
            pl.BlockSpec((r, d), lambda l, j: (0, 0)),
            pl.BlockSpec((1, d, tn), lambda l, j: (l, 0, j)),
            pl.BlockSpec((1, 1, tn), lambda l, j: (l, 0, j)),
        ],
        out_specs=pl.BlockSpec((1, r, tn), lambda l, j: (l, 0, j)),
        out_shape=jax.ShapeDtypeStruct((nl, r, nm), F32),
        compiler_params=_cparams(("arbitrary", "arbitrary")),
        name="modulation",
    )(cc, w_mod, b_mod.reshape(nl, 1, nm))
    return out.reshape(nl, r, N_MOD, d)


def _rms_rows(x):
    return x * lax.rsqrt(jnp.mean(x * x, axis=-1, keepdims=True) + EPS)


def _head_norm(t, gmat, w128, rope_tabs, scale):
    ss = _dot((t * t).astype(BF16), gmat)
    y = t * lax.rsqrt(ss * (1.0 / HEAD_DIM) + EPS)
    outs = []
    for j in range(Q_WIDTH // LANES):
        yj = y[:, j * LANES:(j + 1) * LANES] * w128
        if rope_tabs is not None:
            cos, sa, sb = rope_tabs
            up = pltpu.roll(yj, LANES - 16, 1)
            dn = pltpu.roll(yj, 16, 1)
            yj = yj * cos + up * sa + dn * sb
        outs.append(yj * scale)
    return jnp.concatenate(outs, axis=1).astype(BF16)


def _inproj_kernel(*refs, rope):
    if rope:
        (x_ref, mod_ref, nw_ref, w_ref, wdt_ref, wdtt_ref, g_ref, qw_ref, kw_ref,
         cos_ref, sa_ref, sb_ref, q_o, k_o, v_o, z_o, xbc_o, dt_o, dtt_o) = refs
        tabs = (cos_ref[...], sa_ref[...], sb_ref[...])
    else:
        (x_ref, mod_ref, nw_ref, w_ref, wdt_ref, wdtt_ref, g_ref, qw_ref, kw_ref,
         q_o, k_o, v_o, z_o, xbc_o, dt_o, dtt_o) = refs
        tabs = None
    x = x_ref[0]
    sh = mod_ref[0, 0, 0:1, :]
    sc = mod_ref[0, 0, 1:2, :]
    h = (_rms_rows(x) * nw_ref[0] * (1.0 + sc) + sh).astype(BF16)
    p = _dot(h, w_ref[0])
    gmat = g_ref[...]
    q_o[0] = _head_norm(p[:, 0:512], gmat, qw_ref[0], tabs, QK_SCALE)
    k_o[0] = _head_norm(p[:, 512:1024], gmat, kw_ref[0], tabs, 1.0)
    v_o[0] = p[:, 1024:1536].astype(BF16)
    z_o[0] = p[:, 1536:2048].astype(BF16)
    xbc_o[0] = p[:, 2048:3072].astype(BF16)
    dt_o[0] = _dot(h, wdt_ref[0])
    dtt_o[0] = _dot_nt(wdtt_ref[0], h)


def _in_projection(x, mod, mod_rows, layer, p, rope_tabs, tm):
    b, n, d = x.shape
    rope = rope_tabs is not None
    if mod_rows == 1:
        mod_map = lambda bi, i: (layer, mod.shape[1] - 1, 0, 0)
    else:
        mod_map = lambda bi, i: (layer, bi, 0, 0)
    lsel = lambda bi, i: (layer, 0, 0)
    in_specs = [
        pl.BlockSpec((1, tm, d), lambda bi, i: (bi, i, 0)),
        pl.BlockSpec((1, 1, N_MOD, d), mod_map),
        pl.BlockSpec((1, 1, d), lsel),
        pl.BlockSpec((1, d, P_MAIN), lsel),
        pl.BlockSpec((1, d, N_DT), lsel),
        pl.BlockSpec((1, N_DT, d), lsel),
        pl.BlockSpec((Q_WIDTH, Q_WIDTH), lambda bi, i: (0, 0)),
        pl.BlockSpec((1, 1, LANES), lsel),
        pl.BlockSpec((1, 1, LANES), lsel),
    ]
    args = [x, mod, p["norm1_w"], p["w_main"], p["w_dt"], p["w_dtt"], p["gmat"], p["q_norm_w"], p["k_norm_w"]]
    if rope:
        in_specs += [pl.BlockSpec((tm, LANES), lambda bi, i: (i, 0))] * 3
        args += list(rope_tabs)
    row = lambda w: pl.BlockSpec((1, tm, w), lambda bi, i: (bi, i, 0))
    out_specs = [row(512), row(512), row(512), row(512), row(1024), row(N_DT),
                 pl.BlockSpec((1, N_DT, tm), lambda bi, i: (bi, 0, i))]
    out_shape = [jax.ShapeDtypeStruct((b, n, 512), BF16)] * 4 + [
        jax.ShapeDtypeStruct((b, n, 1024), BF16),
        jax.ShapeDtypeStruct((b, n, N_DT), F32),
        jax.ShapeDtypeStruct((b, N_DT, n), F32)]
    return pl.pallas_call(
        functools.partial(_inproj_kernel, rope=rope),
        grid=(b, n // tm),
        in_specs=in_specs,
        out_specs=out_specs,
        out_shape=out_shape,
        compiler_params=_cparams(("parallel", "arbitrary")),
        name="in_projection_rope" if rope else "in_projection",
    )(*args)


def _attn_kernel(bound_ref, q_ref, k_ref, v_ref, lam_ref, sw_ref, o_ref, *, layer, lam_init):
    tq = q_ref.shape[1]
    lp = lam_ref[0]
    lam = (jnp.exp(jnp.sum(lp[0:1] * lp[1:2], axis=-1, keepdims=True))
           - jnp.exp(jnp.sum(lp[2:3] * lp[3:4], axis=-1, keepdims=True)) + lam_init)
    lane = lax.broadcasted_iota(I32, (1, LANES), 1)
    m_lo = jnp.where(lane < HEAD_DIM, 1.0, 0.0).astype(BF16)
    m_hi = jnp.where(lane < HEAD_DIM, 0.0, 1.0).astype(BF16)
    sw = sw_ref[0]

    def heads(shift_by_max):
        for h in range(ATTN_HEADS):
            cols = slice(h * LANES, (h + 1) * LANES)
            qh = q_ref[0, :, cols]
            qs = jnp.concatenate([qh * m_lo, qh * m_hi], axis=0)
            s = _dot_nt(qs, k_ref[0, :, cols])
            if shift_by_max:
                s = s - jnp.max(s, axis=-1, keepdims=True)
            p = jnp.exp2(s)
            r = 1.0 / jnp.sum(p, axis=-1, keepdims=True)
            pv = _dot(p.astype(BF16), v_ref[0, :, cols]) * r
            o = pv[0:tq] - lam * pv[tq:2 * tq]
            o = _rms_rows(o) * sw * (1.0 - lam_init)
            o_ref[0, :, cols] = o.astype(BF16)

    small = bound_ref[layer] < ATTN_SAFE_LOGIT

    @pl.when(small)
    def _():
        heads(False)

    @pl.when(jnp.logical_not(small))
    def _():
        heads(True)


def _diff_attention(q, k_all, v_all, p, layer, lam_init, tq):
    b, n, _ = q.shape
    nk = k_all.shape[1]
    return pl.pallas_call(
        functools.partial(_attn_kernel, layer=layer, lam_init=lam_init),
        grid=(b, n // tq),
        in_specs=[
            pl.BlockSpec(memory_space=pltpu.SMEM),
            pl.BlockSpec((1, tq, Q_WIDTH), lambda bi, i: (bi, i, 0)),
            pl.BlockSpec((1, nk, Q_WIDTH), lambda bi, i: (bi, 0, 0)),
            pl.BlockSpec((1, nk, ATTN_WIDTH), lambda bi, i: (bi, 0, 0)),
            pl.BlockSpec((1, 4, HEAD_DIM), lambda bi, i: (layer, 0, 0)),
            pl.BlockSpec((1, 1, ATTN_VDIM), lambda bi, i: (layer, 0, 0)),
        ],
        out_specs=pl.BlockSpec((1, tq, ATTN_WIDTH), lambda bi, i: (bi, i, 0)),
        out_shape=jax.ShapeDtypeStruct((b, n, ATTN_WIDTH), BF16),
        compiler_params=_cparams(("parallel", "arbitrary")),
        name="diff_attention",
    )(p["logit_bound"], q, k_all, v_all, p["lam"], p["subln_w"])


def _ssd_kernel(xbc_ref, z_ref, dt_ref, dtt_ref, cw_ref, cb_ref, dtb_ref, dtbt_ref, al_ref, alt_ref,
                dsk_ref, nw_ref, ex_ref, h0_ref, y_ref, hfin_ref, xs_s, c_s, bt_s, yacc_s, ht_s, *, n):
    nc = n // CHUNK
    row = lax.broadcasted_iota(I32, (CHUNK, CHUNK), 0)
    col = lax.broadcasted_iota(I32, (CHUNK, CHUNK), 1)
    tri_lo = col <= row
    tri_up = col >= row
    t_lo = jnp.where(tri_lo, 1.0, 0.0).astype(BF16)
    t_up = jnp.where(tri_up, 1.0, 0.0).astype(BF16)
    a_row = -jnp.exp(al_ref[0])
    a_col = -jnp.exp(alt_ref[0])
    lane512 = lax.broadcasted_iota(I32, (1, SSD_INNER // 2), 1)
    head_masks = [jnp.where((lane512 // SSD_HEADDIM) == r, 1.0, 0.0).astype(BF16) for r in range(SSD_HPG)]

    def chunk_scan(c, d, xs, cm, bt):
        r0 = pl.multiple_of(c * CHUNK, CHUNK)
        dt_c = _softplus(dt_ref[0, pl.ds(r0, CHUNK), :] + dtb_ref[0])
        da_c = dt_c * a_row
        dt_r = _softplus(dtt_ref[0, :, pl.ds(r0, CHUNK)] + dtbt_ref[0])
        da_r = dt_r * a_col
        if d == 0:
            cs_c = _dot3_r(t_lo, da_c)
            cs_r = _dot3_l(da_r, t_up)
            edge = CHUNK - 1
            mask = tri_lo
        else:
            cs_c = _dot3_r(t_up, da_c)
            cs_r = _dot3_l(da_r, t_lo)
            edge = 0
            mask = tri_up
        ex = ex_ref[:, d * SSD_INNER:(d + 1) * SSD_INNER]
        dt_x = _dot3_l(dt_c, ex)
        cs_x = _dot3_l(cs_c, ex)
        cs_edge = cs_x[edge:edge + 1, :]
        xdt = xs * dt_x
        xdt_b = xdt.astype(BF16)
        xw = (xdt * jnp.exp(cs_edge - cs_x)).astype(BF16)
        ecs = jnp.exp(cs_x)
        cdec = jnp.exp(cs_edge)
        ys = []
        for g in range(SSD_GROUPS):
            gl = slice(g * 256, (g + 1) * 256)
            cg = cm[:, g * D_STATE:(g + 1) * D_STATE]
            btg = bt[g * D_STATE:(g + 1) * D_STATE, :]
            cbg = _dot(cg, btg)
            htg = ht_s[d, g]
            y_off = _dot(cg, htg.astype(BF16)) * ecs[:, gl]
            lms, rhs = [], []
            for r in range(SSD_HPG):
                hd = d * SSD_HEADS + g * SSD_HPG + r
                seg = cs_c[:, hd:hd + 1] - cs_r[hd:hd + 1, :]
                dec = jnp.exp(jnp.where(mask, seg, -jnp.inf))
                lms.append((cbg * dec).astype(BF16))
                rhs.append(xdt_b[:, gl] * head_masks[r])
            y_diag = _dot(jnp.concatenate(lms, axis=1), jnp.concatenate(rhs, axis=0))
            ys.append(y_diag + y_off)
            st = _dot(btg, xw[:, gl])
            ht_s[d, g] = htg * cdec[:, gl] + st
        return jnp.concatenate(ys, axis=1)

    ht_s[...] = h0_ref[0]
    cw = cw_ref[0]
    rows1 = lax.broadcasted_iota(I32, (CHUNK, 1), 0)

    def fwd_body(c, carry):
        r0 = pl.multiple_of(c * CHUNK, CHUNK)
        x = xbc_ref[0, pl.ds(r0, CHUNK), :].astype(F32)
        rp0 = pl.multiple_of(jnp.maximum(r0 - 16, 0), 16)
        rn0 = pl.multiple_of(jnp.minimum(r0 + CHUNK, n - 16), 16)
        prev = xbc_ref[0, pl.ds(rp0, 16), :].astype(F32)[15:16, :]
        nxt = xbc_ref[0, pl.ds(rn0, 16), :].astype(F32)[0:1, :]
        prev = jnp.where(c > 0, prev, 0.0)
        nxt = jnp.where(c < nc - 1, nxt, 0.0)
        x_prev = jnp.where(rows1 == 0, prev, pltpu.roll(x, 1, 0))
        x_next = jnp.where(rows1 == CHUNK - 1, nxt, pltpu.roll(x, CHUNK - 1, 0))
        xc = _silu(cb_ref[0] + x_prev * cw[0:1] + x * cw[1:2] + x_next * cw[2:3])
        xs = xc[:, 0:SSD_INNER]
        cm = xc[:, SSD_INNER + 256:SSD_INNER + 512].astype(BF16)
        bm = xc[:, SSD_INNER:SSD_INNER + 256]
        bt = jnp.concatenate([bm[:, 0:D_STATE].T, bm[:, D_STATE:2 * D_STATE].T], axis=0).astype(BF16)
        xs_s[pl.ds(r0, CHUNK), :] = xs
        c_s[pl.ds(r0, CHUNK), :] = cm
        bt_s[:, pl.ds(r0, CHUNK)] = bt
        yacc_s[pl.ds(r0, CHUNK), :] = chunk_scan(c, 0, xs, cm, bt)
        return carry

    lax.fori_loop(0, nc, fwd_body, 0)

    def bwd_body(i, carry):
        c = nc - 1 - i
        r0 = pl.multiple_of(c * CHUNK, CHUNK)
        xs = xs_s[pl.ds(r0, CHUNK), :]
        cm = c_s[pl.ds(r0, CHUNK), :]
        bt = bt_s[:, pl.ds(r0, CHUNK)]
        y = yacc_s[pl.ds(r0, CHUNK), :] + chunk_scan(c, 1, xs, cm, bt) + dsk_ref[0] * xs
        zz = z_ref[0, pl.ds(r0, CHUNK), :].astype(F32)
        g = y * _silu(zz)
        outs = []
        for gi in range(SSD_GROUPS):
            outs.append(_rms_rows(g[:, gi * 256:(gi + 1) * 256]))
        y_ref[0, pl.ds(r0, CHUNK), :] = (jnp.concatenate(outs, axis=1) * nw_ref[0]).astype(BF16)
        return carry

    lax.fori_loop(0, nc, bwd_body, 0)
    hfin_ref[0] = ht_s[...]


def _ssd_mixer(xbc, z, dt, dtt, h0, p, layer):
    b, n, _ = xbc.shape
    lsel3 = lambda bi: (layer, 0, 0)
    st_spec = pl.BlockSpec((1, 2, SSD_GROUPS, D_STATE, 256), lambda bi: (bi, 0, 0, 0, 0))
    return pl.pallas_call(
        functools.partial(_ssd_kernel, n=n),
        grid=(b,),
        in_specs=[
            pl.BlockSpec((1, n, CONV_DIM), lambda bi: (bi, 0, 0)),
            pl.BlockSpec((1, n, SSD_INNER), lambda bi: (bi, 0, 0)),
            pl.BlockSpec((1, n, N_DT), lambda bi: (bi, 0, 0)),
            pl.BlockSpec((1, N_DT, n), lambda bi: (bi, 0, 0)),
            pl.BlockSpec((1, 3, CONV_DIM), lsel3),
            pl.BlockSpec((1, 1, CONV_DIM), lsel3),
            pl.BlockSpec((1, 1, N_DT), lsel3),
            pl.BlockSpec((1, N_DT, 1), lsel3),
            pl.BlockSpec((1, 1, N_DT), lsel3),
            pl.BlockSpec((1, N_DT, 1), lsel3),
            pl.BlockSpec((1, 1, SSD_INNER), lsel3),
            pl.BlockSpec((1, 1, SSD_INNER), lsel3),
            pl.BlockSpec((N_DT, 2 * SSD_INNER), lambda bi: (0, 0)),
            st_spec,
        ],
        out_specs=[pl.BlockSpec((1, n, SSD_INNER), lambda bi: (bi, 0, 0)), st_spec],
        out_shape=[jax.ShapeDtypeStruct((b, n, SSD_INNER), BF16),
                   jax.ShapeDtypeStruct((b, 2, SSD_GROUPS, D_STATE, 256), F32)],
        scratch_shapes=[
            pltpu.VMEM((n, SSD_INNER), F32),
            pltpu.VMEM((n, 2 * D_STATE), BF16),
            pltpu.VMEM((2 * D_STATE, n), BF16),
            pltpu.VMEM((n, SSD_INNER), F32),
            pltpu.VMEM((2, SSD_GROUPS, D_STATE, 256), F32),
        ],
        compiler_params=_cparams(("parallel",)),
        name="ssd_mixer",
    )(xbc, z, dt, dtt, p["conv_w"], p["conv_b"], p["dt_bias"], p["dt_bias_t"], p["a_log"], p["a_log_t"],
      p["d_skip_x"], p["ssd_norm_w"], p["expand"], h0)


def _outproj_kernel(a_ref, s_ref, x_ref, mod_ref, wa_ref, ws_ref, nw_ref, wr_ref, xo_ref, h2_ref, aff_ref):
    mix = _dot(a_ref[0], wa_ref[0]) + _dot(s_ref[0], ws_ref[0])
    x = x_ref[0] + mod_ref[0, 0, 2:3, :] * mix
    xo_ref[0] = x
    h2 = (_rms_rows(x) * nw_ref[0] * (1.0 + mod_ref[0, 0, 4:5, :]) + mod_ref[0, 0, 3:4, :]).astype(BF16)
    h2_ref[0] = h2
    lg = _dot(h2, wr_ref[0])
    lgt = lg.T[0:N_EXPERTS, :]
    e = jnp.exp(lgt - jnp.max(lgt, axis=0, keepdims=True))
    aff_ref[0] = e / jnp.sum(e, axis=0, keepdims=True)


def _out_projection(attn, ssd, x, mod, mod_rows, layer, p, tm, group):
    b, n, d = x.shape
    nt = n // tm
    if mod_rows == 1:
        mod_map = lambda bi, i: (layer, mod.shape[1] - 1, 0, 0)
    else:
        mod_map = lambda bi, i: (layer, bi, 0, 0)
    lsel = lambda bi, i: (layer, 0, 0)
    row = lambda w: pl.BlockSpec((1, tm, w), lambda bi, i: (bi, i, 0))
    return pl.pallas_call(
        _outproj_kernel,
        grid=(b, nt),
        in_specs=[row(ATTN_WIDTH), row(SSD_INNER), row(d),
                  pl.BlockSpec((1, 1, N_MOD, d), mod_map),
                  pl.BlockSpec((1, ATTN_WIDTH, d), lsel),
                  pl.BlockSpec((1, SSD_INNER, d), lsel),
                  pl.BlockSpec((1, 1, d), lsel),
                  pl.BlockSpec((1, d, LANES), lsel)],
        out_specs=[row(d), row(d),
                   pl.BlockSpec((1, N_EXPERTS, tm), lambda bi, i: (bi // group, 0, (bi % group) * nt + i))],
        out_shape=[jax.ShapeDtypeStruct((b, n, d), F32),
                   jax.ShapeDtypeStruct((b, n, d), BF16),
                   jax.ShapeDtypeStruct((b // group, N_EXPERTS, group * n), F32)],
        compiler_params=_cparams(("parallel", "arbitrary")),
        name="out_projection",
    )(attn, ssd, x, mod, p["w_out_a"], p["w_out_s"], p["norm2_w"], p["w_router"])


def _route_kernel(aff_ref, slot_ref, *, n, group, cap):
    bits_all = pltpu.bitcast(aff_ref[0], I32)
    li = lax.broadcasted_iota(I32, (LANES, LANES), 0)
    lj = lax.broadcasted_iota(I32, (LANES, LANES), 1)
    strict_up = jnp.where(li < lj, 1.0, 0.0).astype(BF16)
    fcap = float(cap)

    def count(m):
        return jnp.sum(jnp.where(m, 1.0, 0.0), axis=-1, keepdims=True)

    def excl_prefix(m):
        total = jnp.zeros((N_EXPERTS, 1), F32)
        outs = []
        for j in range(n // LANES):
            blk = jnp.where(m[:, j * LANES:(j + 1) * LANES], 1.0, 0.0)
            outs.append(_dot(blk.astype(BF16), strict_up) + total)
            total = total + jnp.sum(blk, axis=-1, keepdims=True)
        return jnp.concatenate(outs, axis=1)

    for s in range(group):
        bits = bits_all[:, s * n:(s + 1) * n]

        def bisect(i, lo):
            cand = lo | (jnp.int32(1) << (30 - i))
            return jnp.where(count(bits >= cand) >= fcap, cand, lo)

        thr = lax.fori_loop(0, 31, bisect, jnp.zeros((N_EXPERTS, 1), I32))
        gt = bits > thr
        eq = bits == thr
        need = fcap - count(gt)
        sel = gt | (eq & (excl_prefix(eq) < need))
        pos = excl_prefix(sel)
        slot_ref[0, :, s * n:(s + 1) * n] = jnp.where(sel, pos.astype(I32) + s * cap, -1)


def _route(aff, n, group, cap):
    ng, _, t = aff.shape
    return pl.pallas_call(
        functools.partial(_route_kernel, n=n, group=group, cap=cap),
        grid=(ng,),
        in_specs=[pl.BlockSpec((1, N_EXPERTS, t), lambda g: (g, 0, 0))],
        out_specs=pl.BlockSpec((1, N_EXPERTS, t), lambda g: (g, 0, 0)),
        out_shape=jax.ShapeDtypeStruct((ng, N_EXPERTS, t), I32),
        compiler_params=_cparams(("parallel",)),
        name="ec_route",
    )(aff)


def _gather_kernel(slot_ref, aff_ref, h_ref, xe_ref, gate_ref, *, slots):
    t = h_ref.shape[1]
    ne = slot_ref.shape[2]
    sio = lax.broadcasted_iota(I32, (slots, t), 0)
    pm, gates = [], []
    for j in range(ne):
        hit = sio == slot_ref[0, 0, j:j + 1, :]
        pm.append(jnp.where(hit, 1.0, 0.0).astype(BF16))
        gates.append(jnp.sum(jnp.where(hit, aff_ref[0, 0, j:j + 1, :], 0.0), axis=-1, keepdims=True))
    xe = _dot(jnp.concatenate(pm, axis=0), h_ref[0])
    for j in range(ne):
        xe_ref[0, j] = xe[j * slots:(j + 1) * slots].astype(BF16)
        gate_ref[0, j] = jnp.broadcast_to(gates[j], (slots, LANES))


def _ffn_kernel(xe_ref, gate_ref, wg_ref, wu_ref, wd_ref, ye_ref):
    nb, _, s, d = xe_ref.shape
    xe = xe_ref[...].reshape(nb * s, d)
    gate = gate_ref[...].reshape(nb * s, LANES)[:, 0:1]
    hid = (_silu(_dot(xe, wg_ref[0, 0])) * _dot(xe, wu_ref[0, 0])).astype(BF16)
    ye = (_dot(hid, wd_ref[0, 0]) * gate).astype(BF16)
    ye_ref[...] = ye.reshape(nb, 1, s, d)


def _scatter_kernel(slot_ref, ye_ref, x_ref, mod_ref, o_ref, *, slots, chunk):
    tt = x_ref.shape[1]
    ne = slot_ref.shape[1]
    sio = lax.broadcasted_iota(I32, (slots, tt), 0)
    acc = jnp.zeros(x_ref.shape[1:], F32)
    for c0 in range(0, ne, chunk):
        pm = [jnp.where(sio == slot_ref[0, j:j + 1, :], 1.0, 0.0).astype(BF16) for j in range(c0, c0 + chunk)]
        ye = ye_ref[0, c0:c0 + chunk].reshape(chunk * slots, ye_ref.shape[3])
        acc = acc + _dot_tn(jnp.concatenate(pm, axis=0), ye)
    o_ref[0] = x_ref[0] + mod_ref[0, 0, 5:6, :] * acc


def _moe_residual(slot, aff, h2, x_mid, mod, mod_rows, p, layer, group):
    b, n, d = h2.shape
    ng, t = b // group, group * n
    cap = EC_CAPACITY * n // N_EXPERTS
    slots = group * cap
    ech = 4
    xe, gate = pl.pallas_call(
        functools.partial(_gather_kernel, slots=slots),
        grid=(ng, N_EXPERTS // ech),
        in_specs=[
            pl.BlockSpec((1, 1, ech, t), lambda g, c: (g, c, 0, 0)),
            pl.BlockSpec((1, 1, ech, t), lambda g, c: (g, c, 0, 0)),
            pl.BlockSpec((1, t, d), lambda g, c: (g, 0, 0)),
        ],
        out_specs=[pl.BlockSpec((1, ech, slots, d), lambda g, c: (g, c, 0, 0)),
                   pl.BlockSpec((1, ech, slots, LANES), lambda g, c: (g, c, 0, 0))],
        out_shape=[jax.ShapeDtypeStruct((ng, N_EXPERTS, slots, d), BF16),
                   jax.ShapeDtypeStruct((ng, N_EXPERTS, slots, LANES), F32)],
        compiler_params=_cparams(("parallel", "arbitrary")),
        name="ec_gather",
    )(slot.reshape(ng, N_EXPERTS // ech, ech, t), aff.reshape(ng, N_EXPERTS // ech, ech, t),
      h2.reshape(ng, t, d))

    nb = min(ng, 2)
    wsel = lambda e, g: (layer, e, 0, 0)
    ye = pl.pallas_call(
        _ffn_kernel,
        grid=(N_EXPERTS, ng // nb),
        in_specs=[
            pl.BlockSpec((nb, 1, slots, d), lambda e, g: (g, e, 0, 0)),
            pl.BlockSpec((nb, 1, slots, LANES), lambda e, g: (g, e, 0, 0)),
            pl.BlockSpec((1, 1, d, D_EXPERT), wsel),
            pl.BlockSpec((1, 1, d, D_EXPERT), wsel),
            pl.BlockSpec((1, 1, D_EXPERT, d), wsel),
        ],
        out_specs=pl.BlockSpec((nb, 1, slots, d), lambda e, g: (g, e, 0, 0)),
        out_shape=jax.ShapeDtypeStruct((ng, N_EXPERTS, slots, d), BF16),
        compiler_params=_cparams(("parallel", "arbitrary")),
        name="ec_ffn",
    )(xe, gate, p["w_gate"], p["w_up"], p["w_down"])

    tt = min(t, 512)
    nt = t // tt
    if mod_rows == 1:
        mod_map = lambda g, i: (layer, mod.shape[1] - 1, 0, 0)
    else:
        mod_map = lambda g, i: (layer, g, 0, 0)
    out = pl.pallas_call(
        functools.partial(_scatter_kernel, slots=slots, chunk=4),
        grid=(ng, nt),
        in_specs=[
            pl.BlockSpec((1, N_EXPERTS, tt), lambda g, i: (g, 0, i)),
            pl.BlockSpec((1, N_EXPERTS, slots, d), lambda g, i: (g, 0, 0, 0)),
            pl.BlockSpec((1, tt, d), lambda g, i: (g, i, 0)),
            pl.BlockSpec((1, 1, N_MOD, d), mod_map),
        ],
        out_specs=pl.BlockSpec((1, tt, d), lambda g, i: (g, i, 0)),
        out_shape=jax.ShapeDtypeStruct((ng, t, d), F32),
        compiler_params=_cparams(("parallel", "arbitrary")),
        name="ec_scatter",
    )(slot, ye, x_mid.reshape(ng, t, d), mod)
    return out.reshape(b, n, d)


def _rope_tables(n):
    rows = n // GRID_W
    pos = np.stack([np.repeat(np.arange(rows), GRID_W), np.tile(np.arange(GRID_W), rows)], axis=-1)
    n_freq = HEAD_DIM // 4
    inv = jnp.asarray(ROPE_BASE, F32) ** (-jnp.arange(n_freq, dtype=F32) / n_freq)
    ang = jnp.asarray(pos, F32)[..., None] * inv
    cos, sin = jnp.cos(ang), jnp.sin(ang)
    zero = jnp.zeros_like(sin)
    cos64 = jnp.concatenate([cos, cos], axis=-1).reshape(n, HEAD_DIM)
    sa64 = jnp.concatenate([-sin, zero], axis=-1).reshape(n, HEAD_DIM)
    sb64 = jnp.concatenate([zero, sin], axis=-1).reshape(n, HEAD_DIM)
    two = lambda t: jnp.concatenate([t, t], axis=-1)
    return two(cos64), two(sa64), two(sb64)


def _prepare(w_in, norm1_w, norm2_w, q_norm_w, k_norm_w, lambda_q1, lambda_k1, lambda_q2, lambda_k2, subln_w,
             conv_w, conv_b, dt_bias_f, dt_bias_b, a_log_f, a_log_b, d_skip, ssd_norm_w, w_out, w_router,
             w_gate, w_up, w_down):
    nl = w_in.shape[0]
    gi = np.arange(Q_WIDTH) // HEAD_DIM
    lane_head = np.concatenate([np.arange(SSD_INNER) // SSD_HEADDIM, SSD_HEADS + np.arange(SSD_INNER) // SSD_HEADDIM])
    dt_bias = jnp.concatenate([dt_bias_f, dt_bias_b], axis=-1)
    a_log = jnp.concatenate([a_log_f, a_log_b], axis=-1)
    tile2 = lambda w: jnp.concatenate([w, w], axis=-1).reshape(nl, 1, LANES)
    return {
        "w_main": w_in[:, :, :P_MAIN].astype(BF16),
        "w_dt": w_in[:, :, P_MAIN:].astype(BF16),
        "w_dtt": jnp.swapaxes(w_in[:, :, P_MAIN:], 1, 2).astype(BF16),
        "gmat": jnp.asarray(gi[:, None] == gi[None, :], BF16),
        "norm1_w": norm1_w.reshape(nl, 1, D_MODEL),
        "norm2_w": norm2_w.reshape(nl, 1, D_MODEL),
        "q_norm_w": tile2(q_norm_w),
        "k_norm_w": tile2(k_norm_w),
        "lam": jnp.stack([lambda_q1, lambda_k1, lambda_q2, lambda_k2], axis=1),
        "subln_w": subln_w.reshape(nl, 1, ATTN_VDIM),
        "logit_bound": LOGIT_MARGIN * HEAD_DIM ** 0.5 * jnp.max(jnp.abs(q_norm_w), axis=-1) * jnp.max(jnp.abs(k_norm_w), axis=-1),
        "conv_w": conv_w,
        "conv_b": conv_b.reshape(nl, 1, CONV_DIM),
        "dt_bias": dt_bias.reshape(nl, 1, N_DT),
        "dt_bias_t": dt_bias.reshape(nl, N_DT, 1),
        "a_log": a_log.reshape(nl, 1, N_DT),
        "a_log_t": a_log.reshape(nl, N_DT, 1),
        "d_skip_x": jnp.repeat(d_skip, SSD_HEADDIM, axis=-1).reshape(nl, 1, SSD_INNER),
        "ssd_norm_w": ssd_norm_w.reshape(nl, 1, SSD_INNER),
        "expand": jnp.asarray(np.arange(N_DT)[:, None] == lane_head[None, :], BF16),
        "w_out_a": w_out[:, :ATTN_WIDTH, :].astype(BF16),
        "w_out_s": w_out[:, ATTN_WIDTH:, :].astype(BF16),
        "w_router": jnp.pad(w_router, ((0, 0), (0, 0), (0, LANES - N_EXPERTS))).astype(BF16),
        "w_gate": w_gate.astype(BF16),
        "w_up": w_up.astype(BF16),
        "w_down": w_down.astype(BF16),
    }


def _ffn(h2, aff, x_mid, mod, mod_rows, layer, p, group):
    n = h2.shape[1]
    slot = _route(aff, n, group, EC_CAPACITY * n // N_EXPERTS)
    return _moe_residual(slot, aff, h2, x_mid, mod, mod_rows, p, layer, group)


def kernel(x, c, ctx, c_ctx, w_mod, b_mod, norm1_w, norm2_w, w_in, q_norm_w, k_norm_w, lambda_q1, lambda_k1,
           lambda_q2, lambda_k2, subln_w, conv_w, conv_b, dt_bias_f, dt_bias_b, a_log_f, a_log_b, d_skip,
           ssd_norm_w, w_out, w_router, w_gate, w_up, w_down):
    b, n, d = x.shape
    m = ctx.shape[1]
    nl = w_in.shape[0]
    assert d == D_MODEL and n % GRID_W == 0 and n % CHUNK == 0 and m % CHUNK == 0
    assert MOE_TOKENS % n == 0 and MOE_TOKENS % m == 0
    g_lat, g_ctx = MOE_TOKENS // n, MOE_TOKENS // m
    assert b % g_lat == 0 and b % g_ctx == 0
    tm_lat, tm_ctx = min(n, 512), min(m, 256)
    tq = min(n, 256)

    p = _prepare(w_in, norm1_w, norm2_w, q_norm_w, k_norm_w, lambda_q1, lambda_k1, lambda_q2, lambda_k2,
                 subln_w, conv_w, conv_b, dt_bias_f, dt_bias_b, a_log_f, a_log_b, d_skip, ssd_norm_w, w_out,
                 w_router, w_gate, w_up, w_down)
    rope = _rope_tables(n)
    mod = _modulation(jnp.concatenate([c, c_ctx[None, :]], axis=0), w_mod, b_mod)
    h_zero = jnp.zeros((b, 2, SSD_GROUPS, D_STATE, 256), F32)

    x_lat, x_ctx = x, ctx
    for l in range(nl):
        update_ctx = l < nl - 1
        lam_init = 0.8 - 0.6 * math.exp(-0.3 * l)
        qc, kc, vc, zc, xbcc, dtc, dttc = _in_projection(x_ctx, mod, 1, l, p, None, tm_ctx)
        q, k, v, z, xbc, dt, dtt = _in_projection(x_lat, mod, b, l, p, rope, tm_lat)
        ssd_c, h_ctx = _ssd_mixer(xbcc, zc, dtc, dttc, h_zero, p, l)
        ssd, _ = _ssd_mixer(xbc, z, dt, dtt, h_ctx, p, l)
        k_all = jnp.concatenate([kc, k], axis=1)
        v_all = jnp.concatenate([vc, v], axis=1)
        attn = _diff_attention(q, k_all, v_all, p, l, lam_init, tq)
        x_mid, h2, aff = _out_projection(attn, ssd, x_lat, mod, b, l, p, tm_lat, g_lat)
        x_lat = _ffn(h2, aff, x_mid, mod, b, l, p, g_lat)
        if update_ctx:
            attn_c = _diff_attention(qc, kc, vc, p, l, lam_init, min(m, 256))
            xc_mid, hc2, aff_c = _out_projection(attn_c, ssd_c, x_ctx, mod, 1, l, p, tm_ctx, g_ctx)
            x_ctx = _ffn(hc2, aff_c, xc_mid, mod, 1, l, p, g_ctx)
    return x_lat
```

```python
import functools
import math

import jax
import jax.numpy as jnp
import numpy as np
from jax import lax
from jax.experimental import pallas as pl
from jax.experimental.pallas import tpu as pltpu

F32 = jnp.float32
BF16 = jnp.bfloat16
I32 = jnp.int32

D_MODEL = 1024
DEPTH = 4
GRID_W = 64
N_MOD = 6
EPS = 1e-6
ATTN_HEADS = 4
HEAD_DIM = 64
ATTN_VDIM = 128
Q_WIDTH = 512
ATTN_WIDTH = 512
ROPE_BASE = 10000.0
SSD_HEADDIM = 64
SSD_GROUPS = 2
SSD_HPG = 4
SSD_HEADS = 8
SSD_INNER = 512
D_STATE = 128
CHUNK = 128
CONV_DIM = 1024
P_MAIN = 3072
N_DT = 2 * SSD_HEADS
N_EXPERTS = 16
EC_CAPACITY = 2
D_EXPERT = 1408
LANES = 128
MXU_WIDTH = 256
MOE_TOKENS = 2048
QK_SCALE = HEAD_DIM ** -0.5 * math.log2(math.e)
LOGIT_MARGIN = 1.03
ATTN_SAFE_LOGIT = 30.0
VMEM_LIMIT = 56 * 1024 * 1024


def _cparams(sem):
    return pltpu.CompilerParams(dimension_semantics=sem, vmem_limit_bytes=VMEM_LIMIT)


def _sigmoid(x):
    return 0.5 * jnp.tanh(0.5 * x) + 0.5


def _silu(x):
    return x * _sigmoid(x)


def _softplus(x):
    return jnp.maximum(x, 0.0) + jnp.log1p(jnp.exp(-jnp.abs(x)))


def _dot(a, b):
    return jnp.dot(a, b, preferred_element_type=F32)


def _dot_nt(a, b):
    return lax.dot_general(a, b, (((1,), (1,)), ((), ())), preferred_element_type=F32)


def _dot_tn(a, b):
    return lax.dot_general(a, b, (((0,), (0,)), ((), ())), preferred_element_type=F32)


def _split3(x):
    hi = x.astype(BF16)
    r = x - hi.astype(F32)
    mid = r.astype(BF16)
    lo = (r - mid.astype(F32)).astype(BF16)
    return hi, mid, lo


def _dot3_l(x, m):
    hi, mid, lo = _split3(x)
    return _dot(hi, m) + _dot(mid, m) + _dot(lo, m)


def _dot3_r(m, x):
    hi, mid, lo = _split3(x)
    return _dot(m, hi) + _dot(m, mid) + _dot(m, lo)


def _mod_kernel(cc_ref, w_ref, b_ref, o_ref):
    s = _silu(cc_ref[...]).astype(BF16)
    o_ref[0] = _dot(s, w_ref[0].astype(BF16)) + b_ref[0]


def _modulation(cc, w_mod, b_mod):
    nl, d, nm = w_mod.shape
    r = cc.shape[0]
    tn = 1536
    out = pl.pallas_call(
        _mod_kernel,
        grid=(nl, nm // tn),
        in_specs=[
            pl.BlockSpec((r, d), lambda l, j: (0, 0)),
            pl.BlockSpec((1, d, tn), lambda l, j: (l, 0, j)),
            pl.BlockSpec((1, 1, tn), lambda l, j: (l, 0, j)),
        ],
        out_specs=pl.BlockSpec((1, r, tn), lambda l, j: (l, 0, j)),
        out_shape=jax.ShapeDtypeStruct((nl, r, nm), F32),
        compiler_params=_cparams(("arbitrary", "arbitrary")),
        name="modulation",
    )(cc, w_mod, b_mod.reshape(nl, 1, nm))
    return out.reshape(nl, r, N_MOD, d)


def _rms_rows(x):
    return x * lax.rsqrt(jnp.mean(x * x, axis=-1, keepdims=True) + EPS)


def _head_norm(t, gmat, w128, rope_tabs, scale):
    t2 = (t * t).astype(BF16)
    gw = gmat.shape[0]
    ss = jnp.concatenate([_dot(t2[:, j:j + gw], gmat) for j in range(0, Q_WIDTH, gw)], axis=1)
    y = t * lax.rsqrt(ss * (1.0 / HEAD_DIM) + EPS)
    outs = []
    for j in range(Q_WIDTH // LANES):
        yj = y[:, j * LANES:(j + 1) * LANES] * w128
        if rope_tabs is not None:
            cos, sa, sb = rope_tabs
            up = pltpu.roll(yj, LANES - 16, 1)
            dn = pltpu.roll(yj, 16, 1)
            yj = yj * cos + up * sa + dn * sb
        outs.append(yj * scale)
    return jnp.concatenate(outs, axis=1).astype(BF16)


def _inproj_kernel(*refs, rope):
    if rope:
        (x_ref, mod_ref, nw_ref, w_ref, wdt_ref, wdtt_ref, g_ref, qw_ref, kw_ref,
         cos_ref, sa_ref, sb_ref, q_o, k_o, v_o, z_o, xbc_o, dt_o, dtt_o) = refs
        tabs = (cos_ref[...], sa_ref[...], sb_ref[...])
    else:
        (x_ref, mod_ref, nw_ref, w_ref, wdt_ref, wdtt_ref, g_ref, qw_ref, kw_ref, _, _,
         q_o, k_o, v_o, z_o, xbc_o, dt_o, dtt_o) = refs
        tabs = None
    x = x_ref[0]
    sh = mod_ref[0, 0, 0:1, :]
    sc = mod_ref[0, 0, 1:2, :]
    h = (_rms_rows(x) * nw_ref[0] * (1.0 + sc) + sh).astype(BF16)
    p = _dot(h, w_ref[0])
    gmat = g_ref[...]
    q_o[0] = _head_norm(p[:, 0:512], gmat, qw_ref[0], tabs, QK_SCALE)
    k_o[0] = _head_norm(p[:, 512:1024], gmat, kw_ref[0], tabs, 1.0)
    v_o[0] = p[:, 1024:1536].astype(BF16)
    z_o[0] = p[:, 1536:2048].astype(BF16)
    xbc_o[0] = p[:, 2048:3072].astype(BF16)
    dt_o[0] = _dot(h, wdt_ref[0])
    dtt_o[0] = _dot_nt(wdtt_ref[0], h)


def _in_projection(x, mod, mod_rows, layer, p, tm, kv_rows, rope_tabs=None, kv=None):
    b, n, d = x.shape
    rope = rope_tabs is not None
    assert rope != (kv is not None)
    kv_blk0 = 0 if rope else (kv_rows - n) // tm
    if mod_rows == 1:
        mod_map = lambda bi, i: (layer, mod.shape[1] - 1, 0, 0)
    else:
        mod_map = lambda bi, i: (layer, bi, 0, 0)
    lsel = lambda bi, i: (layer, 0, 0)
    in_specs = [
        pl.BlockSpec((1, tm, d), lambda bi, i: (bi, i, 0)),
        pl.BlockSpec((1, 1, N_MOD, d), mod_map),
        pl.BlockSpec((1, 1, d), lsel),
        pl.BlockSpec((1, d, P_MAIN), lsel),
        pl.BlockSpec((1, d, N_DT), lsel),
        pl.BlockSpec((1, N_DT, d), lsel),
        pl.BlockSpec((MXU_WIDTH, MXU_WIDTH), lambda bi, i: (0, 0)),
        pl.BlockSpec((1, 1, LANES), lsel),
        pl.BlockSpec((1, 1, LANES), lsel),
    ]
    args = [x, mod, p["norm1_w"], p["w_main"], p["w_dt"], p["w_dtt"], p["gmat"], p["q_norm_w"], p["k_norm_w"]]
    if rope:
        in_specs += [pl.BlockSpec((tm, LANES), lambda bi, i: (i, 0))] * 3
        args += list(rope_tabs)
        aliases = {}
    else:
        in_specs += [pl.BlockSpec(memory_space=pl.ANY)] * 2
        aliases = {len(args): 1, len(args) + 1: 2}
        args += list(kv)
    row = lambda w: pl.BlockSpec((1, tm, w), lambda bi, i: (bi, i, 0))
    kv_row = pl.BlockSpec((1, tm, 512), lambda bi, i: (bi, kv_blk0 + i, 0))
    out_specs = [row(512), kv_row, kv_row, row(512), row(1024), row(N_DT),
                 pl.BlockSpec((1, N_DT, tm), lambda bi, i: (bi, 0, i))]
    out_shape = [jax.ShapeDtypeStruct((b, n, 512), BF16),
                 jax.ShapeDtypeStruct((b, kv_rows, 512), BF16),
                 jax.ShapeDtypeStruct((b, kv_rows, 512), BF16),
                 jax.ShapeDtypeStruct((b, n, 512), BF16),
                 jax.ShapeDtypeStruct((b, n, 1024), BF16),
                 jax.ShapeDtypeStruct((b, n, N_DT), F32),
                 jax.ShapeDtypeStruct((b, N_DT, n), F32)]
    return pl.pallas_call(
        functools.partial(_inproj_kernel, rope=rope),
        grid=(b, n // tm),
        in_specs=in_specs,
        out_specs=out_specs,
        out_shape=out_shape,
        input_output_aliases=aliases,
        compiler_params=_cparams(("parallel", "arbitrary")),
        name="in_projection_rope" if rope else "in_projection",
    )(*args)


def _attn_kernel(bound_ref, q_ref, k_ref, v_ref, lam_ref, sw_ref, o_ref, *, layer, lam_init):
    tq = q_ref.shape[1]
    lp = lam_ref[0]
    lam = (jnp.exp(jnp.sum(lp[0:1] * lp[1:2], axis=-1, keepdims=True))
           - jnp.exp(jnp.sum(lp[2:3] * lp[3:4], axis=-1, keepdims=True)) + lam_init)
    lane = lax.broadcasted_iota(I32, (1, LANES), 1)
    m_lo = jnp.where(lane < HEAD_DIM, 1.0, 0.0).astype(BF16)
    m_hi = jnp.where(lane < HEAD_DIM, 0.0, 1.0).astype(BF16)
    sw = sw_ref[0]

    def heads(shift_by_max):
        for h in range(ATTN_HEADS):
            cols = slice(h * LANES, (h + 1) * LANES)
            qh = q_ref[0, :, cols]
            qs = jnp.concatenate([qh * m_lo, qh * m_hi], axis=0)
            s = _dot_nt(qs, k_ref[0, :, cols])
            if shift_by_max:
                s = s - jnp.max(s, axis=-1, keepdims=True)
            p = jnp.exp2(s)
            l = jnp.sum(p, axis=-1, keepdims=True)
            pb = p.astype(BF16)
            rho = (lam * l[0:tq] / l[tq:2 * tq]).astype(BF16)
            w = pb[0:tq] - pb[tq:2 * tq] * rho
            o = _dot(w, v_ref[0, :, cols]) / l[0:tq]
            o = _rms_rows(o) * sw * (1.0 - lam_init)
            o_ref[0, :, cols] = o.astype(BF16)

    small = bound_ref[layer] < ATTN_SAFE_LOGIT

    @pl.when(small)
    def _():
        heads(False)

    @pl.when(jnp.logical_not(small))
    def _():
        heads(True)


def _diff_attention(q, k_all, v_all, p, layer, lam_init, tq, nk, kv_block):
    b, n, _ = q.shape
    return pl.pallas_call(
        functools.partial(_attn_kernel, layer=layer, lam_init=lam_init),
        grid=(b, n // tq),
        in_specs=[
            pl.BlockSpec(memory_space=pltpu.SMEM),
            pl.BlockSpec((1, tq, Q_WIDTH), lambda bi, i: (bi, i, 0)),
            pl.BlockSpec((1, nk, Q_WIDTH), lambda bi, i: (bi, kv_block, 0)),
            pl.BlockSpec((1, nk, ATTN_WIDTH), lambda bi, i: (bi, kv_block, 0)),
            pl.BlockSpec((1, 4, HEAD_DIM), lambda bi, i: (layer, 0, 0)),
            pl.BlockSpec((1, 1, ATTN_VDIM), lambda bi, i: (layer, 0, 0)),
        ],
        out_specs=pl.BlockSpec((1, tq, ATTN_WIDTH), lambda bi, i: (bi, i, 0)),
        out_shape=jax.ShapeDtypeStruct((b, n, ATTN_WIDTH), BF16),
        compiler_params=_cparams(("parallel", "arbitrary")),
        name="diff_attention",
    )(p["logit_bound"], q, k_all, v_all, p["lam"], p["subln_w"])


def _ssd_kernel(xbc_ref, z_ref, dt_ref, dtt_ref, cw_ref, cb_ref, dtb_ref, dtbt_ref, al_ref, alt_ref,
                dsk_ref, nw_ref, ex_ref, h0_ref, y_ref, hfin_ref, xs_s, c_s, bt_s, yacc_s, ht_s, *, n):
    nc = n // CHUNK
    row = lax.broadcasted_iota(I32, (CHUNK, CHUNK), 0)
    col = lax.broadcasted_iota(I32, (CHUNK, CHUNK), 1)
    tri_lo = col <= row
    tri_up = col >= row
    t_lo = jnp.where(tri_lo, 1.0, 0.0).astype(BF16)
    t_up = jnp.where(tri_up, 1.0, 0.0).astype(BF16)
    a_row = -jnp.exp(al_ref[0])
    a_col = -jnp.exp(alt_ref[0])
    lane512 = lax.broadcasted_iota(I32, (1, SSD_INNER // 2), 1)
    head_masks = [jnp.where((lane512 // SSD_HEADDIM) == r, 1.0, 0.0).astype(BF16) for r in range(SSD_HPG)]

    def chunk_scan(c, d, xs, cm, bt):
        r0 = pl.multiple_of(c * CHUNK, CHUNK)
        dt_c = _softplus(dt_ref[0, pl.ds(r0, CHUNK), :] + dtb_ref[0])
        da_c = dt_c * a_row
        dt_r = _softplus(dtt_ref[0, :, pl.ds(r0, CHUNK)] + dtbt_ref[0])
        da_r = dt_r * a_col
        if d == 0:
            cs_c = _dot3_r(t_lo, da_c)
            cs_r = _dot3_l(da_r, t_up)
            edge = CHUNK - 1
            mask = tri_lo
        else:
            cs_c = _dot3_r(t_up, da_c)
            cs_r = _dot3_l(da_r, t_lo)
            edge = 0
            mask = tri_up
        ex = ex_ref[:, d * SSD_INNER:(d + 1) * SSD_INNER]
        dt_x = _dot3_l(dt_c, ex)
        cs_x = _dot3_l(cs_c, ex)
        cs_edge = cs_x[edge:edge + 1, :]
        xdt = xs * dt_x
        xdt_b = xdt.astype(BF16)
        xw = (xdt * jnp.exp(cs_edge - cs_x)).astype(BF16)
        ecs = jnp.exp(cs_x)
        cdec = jnp.exp(cs_edge)
        ys = []
        for g in range(SSD_GROUPS):
            gl = slice(g * 256, (g + 1) * 256)
            cg = cm[:, g * D_STATE:(g + 1) * D_STATE]
            btg = bt[g * D_STATE:(g + 1) * D_STATE, :]
            cbg = _dot(cg, btg)
            htg = ht_s[d, g]
            y_off = _dot(cg, htg.astype(BF16)) * ecs[:, gl]
            lms, rhs = [], []
            for r in range(SSD_HPG):
                hd = d * SSD_HEADS + g * SSD_HPG + r
                seg = cs_c[:, hd:hd + 1] - cs_r[hd:hd + 1, :]
                dec = jnp.exp(jnp.where(mask, seg, -jnp.inf))
                lms.append((cbg * dec).astype(BF16))
                rhs.append(xdt_b[:, gl] * head_masks[r])
            y_diag = _dot(jnp.concatenate(lms, axis=1), jnp.concatenate(rhs, axis=0))
            ys.append(y_diag + y_off)
            st = _dot(btg, xw[:, gl])
            ht_s[d, g] = htg * cdec[:, gl] + st
        return jnp.concatenate(ys, axis=1)

    ht_s[...] = h0_ref[0]
    cw = cw_ref[0]
    rows1 = lax.broadcasted_iota(I32, (CHUNK, 1), 0)

    def conv_body(c, carry):
        r0 = pl.multiple_of(c * CHUNK, CHUNK)
        x = xbc_ref[0, pl.ds(r0, CHUNK), :].astype(F32)
        rp0 = pl.multiple_of(jnp.maximum(r0 - 16, 0), 16)
        rn0 = pl.multiple_of(jnp.minimum(r0 + CHUNK, n - 16), 16)
        prev = xbc_ref[0, pl.ds(rp0, 16), :].astype(F32)[15:16, :]
        nxt = xbc_ref[0, pl.ds(rn0, 16), :].astype(F32)[0:1, :]
        prev = jnp.where(c > 0, prev, 0.0)
        nxt = jnp.where(c < nc - 1, nxt, 0.0)
        x_prev = jnp.where(rows1 == 0, prev, pltpu.roll(x, 1, 0))
        x_next = jnp.where(rows1 == CHUNK - 1, nxt, pltpu.roll(x, CHUNK - 1, 0))
        xc = _silu(cb_ref[0] + x_prev * cw[0:1] + x * cw[1:2] + x_next * cw[2:3])
        xs = xc[:, 0:SSD_INNER]
        cm = xc[:, SSD_INNER + 256:SSD_INNER + 512].astype(BF16)
        bm = xc[:, SSD_INNER:SSD_INNER + 256]
        bt = jnp.concatenate([bm[:, 0:D_STATE].T, bm[:, D_STATE:2 * D_STATE].T], axis=0).astype(BF16)
        xs_s[pl.ds(r0, CHUNK), :] = xs
        c_s[pl.ds(r0, CHUNK), :] = cm
        bt_s[:, pl.ds(r0, CHUNK)] = bt
        return carry

    lax.fori_loop(0, nc, conv_body, 0)

    def scan_chunk(c, d):
        r0 = pl.multiple_of(c * CHUNK, CHUNK)
        xs = xs_s[pl.ds(r0, CHUNK), :]
        return xs, chunk_scan(c, d, xs, c_s[pl.ds(r0, CHUNK), :], bt_s[:, pl.ds(r0, CHUNK)])

    def finish_chunk(c, d):
        r0 = pl.multiple_of(c * CHUNK, CHUNK)
        xs, y_d = scan_chunk(c, d)
        y = yacc_s[pl.ds(r0, CHUNK), :] + y_d + dsk_ref[0] * xs
        zz = z_ref[0, pl.ds(r0, CHUNK), :].astype(F32)
        g = y * _silu(zz)
        outs = [_rms_rows(g[:, gi * 256:(gi + 1) * 256]) for gi in range(SSD_GROUPS)]
        y_ref[0, pl.ds(r0, CHUNK), :] = (jnp.concatenate(outs, axis=1) * nw_ref[0]).astype(BF16)

    half = nc // 2

    def outer_body(i, carry):
        for c, d in ((i, 0), (nc - 1 - i, 1)):
            r0 = pl.multiple_of(c * CHUNK, CHUNK)
            yacc_s[pl.ds(r0, CHUNK), :] = scan_chunk(c, d)[1]
        return carry

    def inner_body(i, carry):
        finish_chunk(half + i, 0)
        finish_chunk(half - 1 - i, 1)
        return carry

    lax.fori_loop(0, half, outer_body, 0)
    lax.fori_loop(0, half, inner_body, 0)
    hfin_ref[0] = ht_s[...]


def _ssd_mixer(xbc, z, dt, dtt, h0, p, layer):
    b, n, _ = xbc.shape
    lsel3 = lambda bi: (layer, 0, 0)
    st_spec = pl.BlockSpec((1, 2, SSD_GROUPS, D_STATE, 256), lambda bi: (bi, 0, 0, 0, 0))
    return pl.pallas_call(
        functools.partial(_ssd_kernel, n=n),
        grid=(b,),
        in_specs=[
            pl.BlockSpec((1, n, CONV_DIM), lambda bi: (bi, 0, 0)),
            pl.BlockSpec((1, n, SSD_INNER), lambda bi: (bi, 0, 0)),
            pl.BlockSpec((1, n, N_DT), lambda bi: (bi, 0, 0)),
            pl.BlockSpec((1, N_DT, n), lambda bi: (bi, 0, 0)),
            pl.BlockSpec((1, 3, CONV_DIM), lsel3),
            pl.BlockSpec((1, 1, CONV_DIM), lsel3),
            pl.BlockSpec((1, 1, N_DT), lsel3),
            pl.BlockSpec((1, N_DT, 1), lsel3),
            pl.BlockSpec((1, 1, N_DT), lsel3),
            pl.BlockSpec((1, N_DT, 1), lsel3),
            pl.BlockSpec((1, 1, SSD_INNER), lsel3),
            pl.BlockSpec((1, 1, SSD_INNER), lsel3),
            pl.BlockSpec((N_DT, 2 * SSD_INNER), lambda bi: (0, 0)),
            st_spec,
        ],
        out_specs=[pl.BlockSpec((1, n, SSD_INNER), lambda bi: (bi, 0, 0)), st_spec],
        out_shape=[jax.ShapeDtypeStruct((b, n, SSD_INNER), BF16),
                   jax.ShapeDtypeStruct((b, 2, SSD_GROUPS, D_STATE, 256), F32)],
        scratch_shapes=[
            pltpu.VMEM((n, SSD_INNER), F32),
            pltpu.VMEM((n, 2 * D_STATE), BF16),
            pltpu.VMEM((2 * D_STATE, n), BF16),
            pltpu.VMEM((n, SSD_INNER), F32),
            pltpu.VMEM((2, SSD_GROUPS, D_STATE, 256), F32),
        ],
        compiler_params=_cparams(("parallel",)),
        name="ssd_mixer",
    )(xbc, z, dt, dtt, p["conv_w"], p["conv_b"], p["dt_bias"], p["dt_bias_t"], p["a_log"], p["a_log_t"],
      p["d_skip_x"], p["ssd_norm_w"], p["expand"], h0)


def _outproj_kernel(a_ref, s_ref, x_ref, mod_ref, wa_ref, ws_ref, nw_ref, wr_ref, xo_ref, h2_ref, aff_ref):
    mix = _dot(a_ref[0], wa_ref[0]) + _dot(s_ref[0], ws_ref[0])
    x = x_ref[0] + mod_ref[0, 0, 2:3, :] * mix
    xo_ref[0] = x
    h2 = (_rms_rows(x) * nw_ref[0] * (1.0 + mod_ref[0, 0, 4:5, :]) + mod_ref[0, 0, 3:4, :]).astype(BF16)
    h2_ref[0] = h2
    lg = _dot(h2, wr_ref[0])
    lgt = lg.T[0:N_EXPERTS, :]
    e = jnp.exp(lgt - jnp.max(lgt, axis=0, keepdims=True))
    aff_ref[0] = e / jnp.sum(e, axis=0, keepdims=True)


def _out_projection(attn, ssd, x, mod, mod_rows, layer, p, tm, group):
    b, n, d = x.shape
    nt = n // tm
    if mod_rows == 1:
        mod_map = lambda bi, i: (layer, mod.shape[1] - 1, 0, 0)
    else:
        mod_map = lambda bi, i: (layer, bi, 0, 0)
    lsel = lambda bi, i: (layer, 0, 0)
    row = lambda w: pl.BlockSpec((1, tm, w), lambda bi, i: (bi, i, 0))
    return pl.pallas_call(
        _outproj_kernel,
        grid=(b, nt),
        in_specs=[row(ATTN_WIDTH), row(SSD_INNER), row(d),
                  pl.BlockSpec((1, 1, N_MOD, d), mod_map),
                  pl.BlockSpec((1, ATTN_WIDTH, d), lsel),
                  pl.BlockSpec((1, SSD_INNER, d), lsel),
                  pl.BlockSpec((1, 1, d), lsel),
                  pl.BlockSpec((1, d, LANES), lsel)],
        out_specs=[row(d), row(d),
                   pl.BlockSpec((1, N_EXPERTS, tm), lambda bi, i: (bi // group, 0, (bi % group) * nt + i))],
        out_shape=[jax.ShapeDtypeStruct((b, n, d), F32),
                   jax.ShapeDtypeStruct((b, n, d), BF16),
                   jax.ShapeDtypeStruct((b // group, N_EXPERTS, group * n), F32)],
        compiler_params=_cparams(("parallel", "arbitrary")),
        name="out_projection",
    )(attn, ssd, x, mod, p["w_out_a"], p["w_out_s"], p["norm2_w"], p["w_router"])


def _route_kernel(aff_ref, slot_ref, *, n, group, cap):
    bits_all = pltpu.bitcast(aff_ref[0], I32)
    li = lax.broadcasted_iota(I32, (LANES, LANES), 0)
    lj = lax.broadcasted_iota(I32, (LANES, LANES), 1)
    strict_up = jnp.where(li < lj, 1.0, 0.0).astype(BF16)
    fcap = float(cap)

    def count(m):
        return jnp.sum(jnp.where(m, 1.0, 0.0), axis=-1, keepdims=True)

    def excl_prefix(m):
        total = jnp.zeros((N_EXPERTS, 1), F32)
        outs = []
        for j in range(n // LANES):
            blk = jnp.where(m[:, j * LANES:(j + 1) * LANES], 1.0, 0.0)
            outs.append(_dot(blk.astype(BF16), strict_up) + total)
            total = total + jnp.sum(blk, axis=-1, keepdims=True)
        return jnp.concatenate(outs, axis=1)

    for s in range(group):
        bits = bits_all[:, s * n:(s + 1) * n]

        def bisect(i, lo):
            cand = lo | (jnp.int32(1) << (30 - i))
            return jnp.where(count(bits >= cand) >= fcap, cand, lo)

        thr = lax.fori_loop(0, 31, bisect, jnp.zeros((N_EXPERTS, 1), I32))
        gt = bits > thr
        eq = bits == thr
        need = fcap - count(gt)
        sel = gt | (eq & (excl_prefix(eq) < need))
        pos = excl_prefix(sel)
        slot_ref[0, :, s * n:(s + 1) * n] = jnp.where(sel, pos.astype(I32) + s * cap, -1)


def _route(aff, n, group, cap):
    ng, _, t = aff.shape
    return pl.pallas_call(
        functools.partial(_route_kernel, n=n, group=group, cap=cap),
        grid=(ng,),
        in_specs=[pl.BlockSpec((1, N_EXPERTS, t), lambda g: (g, 0, 0))],
        out_specs=pl.BlockSpec((1, N_EXPERTS, t), lambda g: (g, 0, 0)),
        out_shape=jax.ShapeDtypeStruct((ng, N_EXPERTS, t), I32),
        compiler_params=_cparams(("parallel",)),
        name="ec_route",
    )(aff)


def _gather_kernel(slot_ref, aff_ref, h_ref, xe_ref, gate_ref, *, slots):
    t = h_ref.shape[1]
    ne = slot_ref.shape[2]
    sio = lax.broadcasted_iota(I32, (slots, t), 0)
    pm, gates = [], []
    for j in range(ne):
        hit = sio == slot_ref[0, 0, j:j + 1, :]
        pm.append(jnp.where(hit, 1.0, 0.0).astype(BF16))
        gates.append(jnp.sum(jnp.where(hit, aff_ref[0, 0, j:j + 1, :], 0.0), axis=-1, keepdims=True))
    xe = _dot(jnp.concatenate(pm, axis=0), h_ref[0])
    for j in range(ne):
        xe_ref[0, j] = xe[j * slots:(j + 1) * slots].astype(BF16)
        gate_ref[0, j] = jnp.broadcast_to(gates[j], (slots, LANES))


def _ffn_kernel(xe_ref, gate_ref, wg_ref, wu_ref, wd_ref, ye_ref):
    nb, _, s, d = xe_ref.shape
    xe = xe_ref[...].reshape(nb * s, d)
    gate = gate_ref[...].reshape(nb * s, LANES)[:, 0:1]
    hid = (_silu(_dot(xe, wg_ref[0, 0])) * _dot(xe, wu_ref[0, 0])).astype(BF16)
    ye = (_dot(hid, wd_ref[0, 0]) * gate).astype(BF16)
    ye_ref[...] = ye.reshape(nb, 1, s, d)


def _scatter_kernel(slot_ref, ye_ref, x_ref, mod_ref, o_ref, *, slots, chunk):
    tt = x_ref.shape[1]
    ne = slot_ref.shape[1]
    sio = lax.broadcasted_iota(I32, (slots, tt), 0)
    acc = jnp.zeros(x_ref.shape[1:], F32)
    for c0 in range(0, ne, chunk):
        pm = [jnp.where(sio == slot_ref[0, j:j + 1, :], 1.0, 0.0).astype(BF16) for j in range(c0, c0 + chunk)]
        ye = ye_ref[0, c0:c0 + chunk].reshape(chunk * slots, ye_ref.shape[3])
        acc = acc + _dot_tn(jnp.concatenate(pm, axis=0), ye)
    o_ref[0] = x_ref[0] + mod_ref[0, 0, 5:6, :] * acc


def _moe_residual(slot, aff, h2, x_mid, mod, mod_rows, p, layer, group):
    b, n, d = h2.shape
    ng, t = b // group, group * n
    cap = EC_CAPACITY * n // N_EXPERTS
    slots = group * cap
    ech = 4
    xe, gate = pl.pallas_call(
        functools.partial(_gather_kernel, slots=slots),
        grid=(ng, N_EXPERTS // ech),
        in_specs=[
            pl.BlockSpec((1, 1, ech, t), lambda g, c: (g, c, 0, 0)),
            pl.BlockSpec((1, 1, ech, t), lambda g, c: (g, c, 0, 0)),
            pl.BlockSpec((1, t, d), lambda g, c: (g, 0, 0)),
        ],
        out_specs=[pl.BlockSpec((1, ech, slots, d), lambda g, c: (g, c, 0, 0)),
                   pl.BlockSpec((1, ech, slots, LANES), lambda g, c: (g, c, 0, 0))],
        out_shape=[jax.ShapeDtypeStruct((ng, N_EXPERTS, slots, d), BF16),
                   jax.ShapeDtypeStruct((ng, N_EXPERTS, slots, LANES), F32)],
        compiler_params=_cparams(("parallel", "arbitrary")),
        name="ec_gather",
    )(slot.reshape(ng, N_EXPERTS // ech, ech, t), aff.reshape(ng, N_EXPERTS // ech, ech, t),
      h2.reshape(ng, t, d))

    nb = min(ng, 2)
    wsel = lambda e, g: (layer, e, 0, 0)
    ye = pl.pallas_call(
        _ffn_kernel,
        grid=(N_EXPERTS, ng // nb),
        in_specs=[
            pl.BlockSpec((nb, 1, slots, d), lambda e, g: (g, e, 0, 0)),
            pl.BlockSpec((nb, 1, slots, LANES), lambda e, g: (g, e, 0, 0)),
            pl.BlockSpec((1, 1, d, D_EXPERT), wsel),
            pl.BlockSpec((1, 1, d, D_EXPERT), wsel),
            pl.BlockSpec((1, 1, D_EXPERT, d), wsel),
        ],
        out_specs=pl.BlockSpec((nb, 1, slots, d), lambda e, g: (g, e, 0, 0)),
        out_shape=jax.ShapeDtypeStruct((ng, N_EXPERTS, slots, d), BF16),
        compiler_params=_cparams(("parallel", "arbitrary")),
        name="ec_ffn",
    )(xe, gate, p["w_gate"], p["w_up"], p["w_down"])

    tt = min(t, 512)
    nt = t // tt
    if mod_rows == 1:
        mod_map = lambda g, i: (layer, mod.shape[1] - 1, 0, 0)
    else:
        mod_map = lambda g, i: (layer, g, 0, 0)
    out = pl.pallas_call(
        functools.partial(_scatter_kernel, slots=slots, chunk=4),
        grid=(ng, nt),
        in_specs=[
            pl.BlockSpec((1, N_EXPERTS, tt), lambda g, i: (g, 0, i)),
            pl.BlockSpec((1, N_EXPERTS, slots, d), lambda g, i: (g, 0, 0, 0)),
            pl.BlockSpec((1, tt, d), lambda g, i: (g, i, 0)),
            pl.BlockSpec((1, 1, N_MOD, d), mod_map),
        ],
        out_specs=pl.BlockSpec((1, tt, d), lambda g, i: (g, i, 0)),
        out_shape=jax.ShapeDtypeStruct((ng, t, d), F32),
        compiler_params=_cparams(("parallel", "arbitrary")),
        name="ec_scatter",
    )(slot, ye, x_mid.reshape(ng, t, d), mod)
    return out.reshape(b, n, d)


def _rope_tables(n):
    rows = n // GRID_W
    pos = np.stack([np.repeat(np.arange(rows), GRID_W), np.tile(np.arange(GRID_W), rows)], axis=-1)
    n_freq = HEAD_DIM // 4
    inv = jnp.asarray(ROPE_BASE, F32) ** (-jnp.arange(n_freq, dtype=F32) / n_freq)
    ang = jnp.asarray(pos, F32)[..., None] * inv
    cos, sin = jnp.cos(ang), jnp.sin(ang)
    zero = jnp.zeros_like(sin)
    cos64 = jnp.concatenate([cos, cos], axis=-1).reshape(n, HEAD_DIM)
    sa64 = jnp.concatenate([-sin, zero], axis=-1).reshape(n, HEAD_DIM)
    sb64 = jnp.concatenate([zero, sin], axis=-1).reshape(n, HEAD_DIM)
    two = lambda t: jnp.concatenate([t, t], axis=-1)
    return two(cos64), two(sa64), two(sb64)


def _prepare(w_in, norm1_w, norm2_w, q_norm_w, k_norm_w, lambda_q1, lambda_k1, lambda_q2, lambda_k2, subln_w,
             conv_w, conv_b, dt_bias_f, dt_bias_b, a_log_f, a_log_b, d_skip, ssd_norm_w, w_out, w_router,
             w_gate, w_up, w_down):
    nl = w_in.shape[0]
    gi = np.arange(MXU_WIDTH) // HEAD_DIM
    lane_head = np.concatenate([np.arange(SSD_INNER) // SSD_HEADDIM, SSD_HEADS + np.arange(SSD_INNER) // SSD_HEADDIM])
    dt_bias = jnp.concatenate([dt_bias_f, dt_bias_b], axis=-1)
    a_log = jnp.concatenate([a_log_f, a_log_b], axis=-1)
    tile2 = lambda w: jnp.concatenate([w, w], axis=-1).reshape(nl, 1, LANES)
    return {
        "w_main": w_in[:, :, :P_MAIN].astype(BF16),
        "w_dt": w_in[:, :, P_MAIN:].astype(BF16),
        "w_dtt": jnp.swapaxes(w_in[:, :, P_MAIN:], 1, 2).astype(BF16),
        "gmat": jnp.asarray(gi[:, None] == gi[None, :], BF16),
        "norm1_w": norm1_w.reshape(nl, 1, D_MODEL),
        "norm2_w": norm2_w.reshape(nl, 1, D_MODEL),
        "q_norm_w": tile2(q_norm_w),
        "k_norm_w": tile2(k_norm_w),
        "lam": jnp.stack([lambda_q1, lambda_k1, lambda_q2, lambda_k2], axis=1),
        "subln_w": subln_w.reshape(nl, 1, ATTN_VDIM),
        "logit_bound": LOGIT_MARGIN * HEAD_DIM ** 0.5 * jnp.max(jnp.abs(q_norm_w), axis=-1) * jnp.max(jnp.abs(k_norm_w), axis=-1),
        "conv_w": conv_w,
        "conv_b": conv_b.reshape(nl, 1, CONV_DIM),
        "dt_bias": dt_bias.reshape(nl, 1, N_DT),
        "dt_bias_t": dt_bias.reshape(nl, N_DT, 1),
        "a_log": a_log.reshape(nl, 1, N_DT),
        "a_log_t": a_log.reshape(nl, N_DT, 1),
        "d_skip_x": jnp.repeat(d_skip, SSD_HEADDIM, axis=-1).reshape(nl, 1, SSD_INNER),
        "ssd_norm_w": ssd_norm_w.reshape(nl, 1, SSD_INNER),
        "expand": jnp.asarray(np.arange(N_DT)[:, None] == lane_head[None, :], BF16),
        "w_out_a": w_out[:, :ATTN_WIDTH, :].astype(BF16),
        "w_out_s": w_out[:, ATTN_WIDTH:, :].astype(BF16),
        "w_router": jnp.pad(w_router, ((0, 0), (0, 0), (0, LANES - N_EXPERTS))).astype(BF16),
        "w_gate": w_gate.astype(BF16),
        "w_up": w_up.astype(BF16),
        "w_down": w_down.astype(BF16),
    }


def _ffn(h2, aff, x_mid, mod, mod_rows, layer, p, group):
    n = h2.shape[1]
    slot = _route(aff, n, group, EC_CAPACITY * n // N_EXPERTS)
    return _moe_residual(slot, aff, h2, x_mid, mod, mod_rows, p, layer, group)


def kernel(x, c, ctx, c_ctx, w_mod, b_mod, norm1_w, norm2_w, w_in, q_norm_w, k_norm_w, lambda_q1, lambda_k1,
           lambda_q2, lambda_k2, subln_w, conv_w, conv_b, dt_bias_f, dt_bias_b, a_log_f, a_log_b, d_skip,
           ssd_norm_w, w_out, w_router, w_gate, w_up, w_down):
    b, n, d = x.shape
    m = ctx.shape[1]
    nl = w_in.shape[0]
    assert d == D_MODEL and n % GRID_W == 0 and n % (2 * CHUNK) == 0 and m % (2 * CHUNK) == 0 and n % m == 0
    assert MOE_TOKENS % n == 0 and MOE_TOKENS % m == 0
    g_lat, g_ctx = MOE_TOKENS // n, MOE_TOKENS // m
    assert b % g_lat == 0 and b % g_ctx == 0
    tm_lat, tm_ctx = min(n, 512), min(m, 256)
    tq = min(n, 256)

    p = _prepare(w_in, norm1_w, norm2_w, q_norm_w, k_norm_w, lambda_q1, lambda_k1, lambda_q2, lambda_k2,
                 subln_w, conv_w, conv_b, dt_bias_f, dt_bias_b, a_log_f, a_log_b, d_skip, ssd_norm_w, w_out,
                 w_router, w_gate, w_up, w_down)
    rope = _rope_tables(n)
    mod = _modulation(jnp.concatenate([c, c_ctx[None, :]], axis=0), w_mod, b_mod)
    h_zero = jnp.zeros((b, 2, SSD_GROUPS, D_STATE, 256), F32)

    x_lat, x_ctx = x, ctx
    for l in range(nl):
        update_ctx = l < nl - 1
        lam_init = 0.8 - 0.6 * math.exp(-0.3 * l)
        q, k_all, v_all, z, xbc, dt, dtt = _in_projection(x_lat, mod, b, l, p, tm_lat, n + m, rope_tabs=rope)
        qc, k_all, v_all, zc, xbcc, dtc, dttc = _in_projection(x_ctx, mod, 1, l, p, tm_ctx, n + m, kv=(k_all, v_all))
        ssd_c, h_ctx = _ssd_mixer(xbcc, zc, dtc, dttc, h_zero, p, l)
        ssd, _ = _ssd_mixer(xbc, z, dt, dtt, h_ctx, p, l)
        attn = _diff_attention(q, k_all, v_all, p, l, lam_init, tq, n + m, 0)
        x_mid, h2, aff = _out_projection(attn, ssd, x_lat, mod, b, l, p, tm_lat, g_lat)
        x_lat = _ffn(h2, aff, x_mid, mod, b, l, p, g_lat)
        if update_ctx:
            attn_c = _diff_attention(qc, k_all, v_all, p, l, lam_init, min(m, 256), m, n // m)
            xc_mid, hc2, aff_c = _out_projection(attn_c, ssd_c, x_ctx, mod, 1, l, p, tm_ctx, g_ctx)
            x_ctx = _ffn(hc2, aff_c, xc_mid, mod, 1, l, p, g_ctx)
    return x_lat
```

```python
import functools
import math

import jax
import jax.numpy as jnp
import numpy as np
from jax import lax
from jax.experimental import pallas as pl
from jax.experimental.pallas import tpu as pltpu

F32 = jnp.float32
BF16 = jnp.bfloat16
I32 = jnp.int32

D_MODEL = 1024
DEPTH = 4
GRID_W = 64
N_MOD = 6
EPS = 1e-6
ATTN_HEADS = 4
HEAD_DIM = 64
ATTN_VDIM = 128
Q_WIDTH = 512
ATTN_WIDTH = 512
ROPE_BASE = 10000.0
SSD_HEADDIM = 64
SSD_GROUPS = 2
SSD_HPG = 4
SSD_HEADS = 8
SSD_INNER = 512
D_STATE = 128
CHUNK = 128
CONV_DIM = 1024
P_MAIN = 3072
N_DT = 2 * SSD_HEADS
N_EXPERTS = 16
EC_CAPACITY = 2
D_EXPERT = 1408
LANES = 128
MXU_WIDTH = 256
MOE_TOKENS = 2048
QK_SCALE = HEAD_DIM ** -0.5 * math.log2(math.e)
LOGIT_MARGIN = 1.03
ATTN_SAFE_LOGIT = 30.0
VMEM_LIMIT = 56 * 1024 * 1024


def _cparams(sem):
    return pltpu.CompilerParams(dimension_semantics=sem, vmem_limit_bytes=VMEM_LIMIT)


def _sigmoid(x):
    return 0.5 * jnp.tanh(0.5 * x) + 0.5


def _silu(x):
    return x * _sigmoid(x)


def _softplus(x):
    return jnp.maximum(x, 0.0) + jnp.log1p(jnp.exp(-jnp.abs(x)))


def _dot(a, b):
    return jnp.dot(a, b, preferred_element_type=F32)


def _dot_nt(a, b):
    return lax.dot_general(a, b, (((1,), (1,)), ((), ())), preferred_element_type=F32)


def _dot_tn(a, b):
    return lax.dot_general(a, b, (((0,), (0,)), ((), ())), preferred_element_type=F32)


def _split3(x):
    hi = x.astype(BF16)
    r = x - hi.astype(F32)
    mid = r.astype(BF16)
    lo = (r - mid.astype(F32)).astype(BF16)
    return hi, mid, lo


def _mod_kernel(cc_ref, w_ref, b_ref, o_ref):
    s = _silu(cc_ref[...]).astype(BF16)
    o_ref[0] = _dot(s, w_ref[0].astype(BF16)) + b_ref[0]


def _modulation(cc, w_mod, b_mod):
    nl, d, nm = w_mod.shape
    r = cc.shape[0]
    tn = 1536
    out = pl.pallas_call(
        _mod_kernel,
        grid=(nl, nm // tn),
        in_specs=[
            pl.BlockSpec((r, d), lambda l, j: (0, 0)),
            pl.BlockSpec((1, d, tn), lambda l, j: (l, 0, j)),
            pl.BlockSpec((1, 1, tn), lambda l, j: (l, 0, j)),
        ],
        out_specs=pl.BlockSpec((1, r, tn), lambda l, j: (l, 0, j)),
        out_shape=jax.ShapeDtypeStruct((nl, r, nm), F32),
        compiler_params=_cparams(("arbitrary", "arbitrary")),
        name="modulation",
    )(cc, w_mod, b_mod.reshape(nl, 1, nm))
    return out.reshape(nl, r, N_MOD, d)


def _rms_rows(x):
    return x * lax.rsqrt(jnp.mean(x * x, axis=-1, keepdims=True) + EPS)


def _head_norm(t, gmat, w128, rope_tabs, scale):
    t2 = (t * t).astype(BF16)
    gw = gmat.shape[0]
    ss = jnp.concatenate([_dot(t2[:, j:j + gw], gmat) for j in range(0, Q_WIDTH, gw)], axis=1)
    y = t * lax.rsqrt(ss * (1.0 / HEAD_DIM) + EPS)
    outs = []
    for j in range(Q_WIDTH // LANES):
        yj = y[:, j * LANES:(j + 1) * LANES] * w128
        if rope_tabs is not None:
            cos, sa, sb = rope_tabs
            up = pltpu.roll(yj, LANES - 16, 1)
            dn = pltpu.roll(yj, 16, 1)
            yj = yj * cos + up * sa + dn * sb
        outs.append(yj * scale)
    return jnp.concatenate(outs, axis=1).astype(BF16)


def _inproj_kernel(*refs, rope):
    if rope:
        (x_ref, mod_ref, nw_ref, w_ref, wdtt_ref, g_ref, qw_ref, kw_ref,
         cos_ref, sa_ref, sb_ref, q_o, k_o, v_o, z_o, xbc_o, dtt_o) = refs
        tabs = (cos_ref[...], sa_ref[...], sb_ref[...])
    else:
        (x_ref, mod_ref, nw_ref, w_ref, wdtt_ref, g_ref, qw_ref, kw_ref, _, _,
         q_o, k_o, v_o, z_o, xbc_o, dtt_o) = refs
        tabs = None
    x = x_ref[0]
    sh = mod_ref[0, 0, 0:1, :]
    sc = mod_ref[0, 0, 1:2, :]
    h = (_rms_rows(x) * nw_ref[0] * (1.0 + sc) + sh).astype(BF16)
    p = _dot(h, w_ref[0])
    gmat = g_ref[...]
    q_o[0] = _head_norm(p[:, 0:512], gmat, qw_ref[0], tabs, QK_SCALE)
    k_o[0] = _head_norm(p[:, 512:1024], gmat, kw_ref[0], tabs, 1.0)
    v_o[0] = p[:, 1024:1536].astype(BF16)
    z_o[0] = p[:, 1536:2048].astype(BF16)
    xbc_o[0] = p[:, 2048:3072].astype(BF16)
    dtt_o[0] = _dot_nt(wdtt_ref[0], h)


def _in_projection(x, mod, mod_rows, layer, p, tm, kv_rows, rope_tabs=None, kv=None):
    b, n, d = x.shape
    rope = rope_tabs is not None
    assert rope != (kv is not None)
    kv_blk0 = 0 if rope else (kv_rows - n) // tm
    if mod_rows == 1:
        mod_map = lambda bi, i: (layer, mod.shape[1] - 1, 0, 0)
    else:
        mod_map = lambda bi, i: (layer, bi, 0, 0)
    lsel = lambda bi, i: (layer, 0, 0)
    in_specs = [
        pl.BlockSpec((1, tm, d), lambda bi, i: (bi, i, 0)),
        pl.BlockSpec((1, 1, N_MOD, d), mod_map),
        pl.BlockSpec((1, 1, d), lsel),
        pl.BlockSpec((1, d, P_MAIN), lsel),
        pl.BlockSpec((1, N_DT, d), lsel),
        pl.BlockSpec((MXU_WIDTH, MXU_WIDTH), lambda bi, i: (0, 0)),
        pl.BlockSpec((1, 1, LANES), lsel),
        pl.BlockSpec((1, 1, LANES), lsel),
    ]
    args = [x, mod, p["norm1_w"], p["w_main"], p["w_dtt"], p["gmat"], p["q_norm_w"], p["k_norm_w"]]
    if rope:
        in_specs += [pl.BlockSpec((tm, LANES), lambda bi, i: (i, 0))] * 3
        args += list(rope_tabs)
        aliases = {}
    else:
        in_specs += [pl.BlockSpec(memory_space=pl.ANY)] * 2
        aliases = {len(args): 1, len(args) + 1: 2}
        args += list(kv)
    row = lambda w: pl.BlockSpec((1, tm, w), lambda bi, i: (bi, i, 0))
    kv_row = pl.BlockSpec((1, tm, 512), lambda bi, i: (bi, kv_blk0 + i, 0))
    out_specs = [row(512), kv_row, kv_row, row(512), row(1024),
                 pl.BlockSpec((1, N_DT, tm), lambda bi, i: (bi, 0, i))]
    out_shape = [jax.ShapeDtypeStruct((b, n, 512), BF16),
                 jax.ShapeDtypeStruct((b, kv_rows, 512), BF16),
                 jax.ShapeDtypeStruct((b, kv_rows, 512), BF16),
                 jax.ShapeDtypeStruct((b, n, 512), BF16),
                 jax.ShapeDtypeStruct((b, n, 1024), BF16),
                 jax.ShapeDtypeStruct((b, N_DT, n), F32)]
    return pl.pallas_call(
        functools.partial(_inproj_kernel, rope=rope),
        grid=(b, n // tm),
        in_specs=in_specs,
        out_specs=out_specs,
        out_shape=out_shape,
        input_output_aliases=aliases,
        compiler_params=_cparams(("parallel", "arbitrary")),
        name="in_projection_rope" if rope else "in_projection",
    )(*args)


def _attn_kernel(bound_ref, q_ref, k_ref, v_ref, lam_ref, sw_ref, o_ref, *, layer, lam_init):
    tq = q_ref.shape[1]
    lp = lam_ref[0]
    lam = (jnp.exp(jnp.sum(lp[0:1] * lp[1:2], axis=-1, keepdims=True))
           - jnp.exp(jnp.sum(lp[2:3] * lp[3:4], axis=-1, keepdims=True)) + lam_init)
    lane = lax.broadcasted_iota(I32, (1, LANES), 1)
    m_lo = jnp.where(lane < HEAD_DIM, 1.0, 0.0).astype(BF16)
    m_hi = jnp.where(lane < HEAD_DIM, 0.0, 1.0).astype(BF16)
    sw = sw_ref[0]

    def heads(shift_by_max):
        for h in range(ATTN_HEADS):
            cols = slice(h * LANES, (h + 1) * LANES)
            qh = q_ref[0, :, cols]
            qs = jnp.concatenate([qh * m_lo, qh * m_hi], axis=0)
            s = _dot_nt(qs, k_ref[0, :, cols])
            if shift_by_max:
                s = s - jnp.max(s, axis=-1, keepdims=True)
            p = jnp.exp2(s)
            r = 1.0 / jnp.sum(p, axis=-1, keepdims=True)
            pv = _dot(p.astype(BF16), v_ref[0, :, cols]) * r
            o = pv[0:tq] - lam * pv[tq:2 * tq]
            o = _rms_rows(o) * sw * (1.0 - lam_init)
            o_ref[0, :, cols] = o.astype(BF16)

    small = bound_ref[layer] < ATTN_SAFE_LOGIT

    @pl.when(small)
    def _():
        heads(False)

    @pl.when(jnp.logical_not(small))
    def _():
        heads(True)


def _diff_attention(q, k_all, v_all, p, layer, lam_init, tq, nk, kv_block):
    b, n, _ = q.shape
    return pl.pallas_call(
        functools.partial(_attn_kernel, layer=layer, lam_init=lam_init),
        grid=(b, n // tq),
        in_specs=[
            pl.BlockSpec(memory_space=pltpu.SMEM),
            pl.BlockSpec((1, tq, Q_WIDTH), lambda bi, i: (bi, i, 0)),
            pl.BlockSpec((1, nk, Q_WIDTH), lambda bi, i: (bi, kv_block, 0)),
            pl.BlockSpec((1, nk, ATTN_WIDTH), lambda bi, i: (bi, kv_block, 0)),
            pl.BlockSpec((1, 4, HEAD_DIM), lambda bi, i: (layer, 0, 0)),
            pl.BlockSpec((1, 1, ATTN_VDIM), lambda bi, i: (layer, 0, 0)),
        ],
        out_specs=pl.BlockSpec((1, tq, ATTN_WIDTH), lambda bi, i: (bi, i, 0)),
        out_shape=jax.ShapeDtypeStruct((b, n, ATTN_WIDTH), BF16),
        compiler_params=_cparams(("parallel", "arbitrary")),
        name="diff_attention",
    )(p["logit_bound"], q, k_all, v_all, p["lam"], p["subln_w"])


def _ssd_kernel(xbc_ref, z_ref, dtt_ref, cw_ref, cb_ref, dtbt_ref, alt_ref, dsk_ref, nw_ref, sp_ref, h0_ref,
                y_ref, hfin_ref, xs_s, c_s, bt_s, yf_s, yb_s, hf0_s, hf1_s, hb0_s, hb1_s, *, n):
    nc = n // CHUNK
    ht_s = ((hf0_s, hf1_s), (hb0_s, hb1_s))
    yacc_s = (yf_s, yb_s)
    row = lax.broadcasted_iota(I32, (CHUNK, CHUNK), 0)
    col = lax.broadcasted_iota(I32, (CHUNK, CHUNK), 1)
    tri_lo = col <= row
    tri_up = col >= row
    t_lo = jnp.where(tri_lo, 1.0, 0.0).astype(BF16)
    t_up = jnp.where(tri_up, 1.0, 0.0).astype(BF16)
    a_col = -jnp.exp(alt_ref[0])
    lane_head = lax.broadcasted_iota(I32, (1, SSD_INNER // 2), 1) // SSD_HEADDIM
    head_masks_f = [jnp.where(lane_head == r, 1.0, 0.0) for r in range(SSD_HPG)]
    head_masks = [m.astype(BF16) for m in head_masks_f]
    zero_rows = jnp.zeros((CHUNK - 6 * SSD_HEADS, CHUNK), F32)
    spread_mat = sp_ref[...]


    def scan_sums(c, d):
        r0 = pl.multiple_of(c * CHUNK, CHUNK)
        hs = slice(d * SSD_HEADS, (d + 1) * SSD_HEADS)
        dt_r = _softplus(dtt_ref[0, hs, pl.ds(r0, CHUNK)] + dtbt_ref[0, hs, :])
        hi, mid, lo = _split3(dt_r * a_col[hs])
        da3 = jnp.concatenate([hi.astype(F32), mid.astype(F32), lo.astype(F32)], axis=0).astype(BF16)
        cs3 = _dot(da3, t_up if d == 0 else t_lo)
        cs_r = cs3[0:8] + cs3[8:16] + cs3[16:24]
        edge = CHUNK - 1 if d == 0 else 0
        cs_edge = cs_r[:, edge:edge + 1]
        return dict(c=c, d=d, r0=r0, dt_r=dt_r, cs_r=cs_r,
                    wst_r=(dt_r * jnp.exp(cs_edge - cs_r)).astype(BF16),
                    cdec=jnp.exp(cs_edge))

    def scan_spread(st):
        cs_r, r0 = st["cs_r"], st["r0"]
        pieces = [t.astype(F32) for t in _split3(cs_r) + _split3(jnp.exp(cs_r))]
        colf = jnp.concatenate(pieces + [zero_rows], axis=0).T.astype(BF16)
        st["spread"] = _dot(colf, spread_mat)
        st["xs"] = xs_s[pl.ds(r0, CHUNK), :]
        st["xs_b"] = st["xs"].astype(BF16)
        st["cm"] = c_s[pl.ds(r0, CHUNK), :]
        st["bt"] = bt_s[:, pl.ds(r0, CHUNK)]
        st["ys"] = []

    def scan_group(st, g):
        d, cs_r, dt_r, spread = st["d"], st["cs_r"], st["dt_r"], st["spread"]
        mask = tri_lo if d == 0 else tri_up
        gl = slice(g * 256, (g + 1) * 256)
        cg = st["cm"][:, g * D_STATE:(g + 1) * D_STATE]
        btg = st["bt"][g * D_STATE:(g + 1) * D_STATE, :]
        cbg = _dot(cg, btg)
        htg = ht_s[d][g][...]
        y_off = _dot(cg, htg.astype(BF16)) * spread[:, SSD_HEADS * CHUNK + g * 256:SSD_HEADS * CHUNK + (g + 1) * 256]
        lms, bts, rstack = [], [], []
        cdec_x = jnp.zeros((1, 256), F32)
        for r in range(SSD_HPG):
            hh = g * SSD_HPG + r
            seg = spread[:, hh * CHUNK:(hh + 1) * CHUNK] - cs_r[hh:hh + 1, :]
            dec = jnp.exp(jnp.where(mask, seg, -jnp.inf))
            lms.append((cbg * dec * dt_r[hh:hh + 1, :]).astype(BF16))
            bts.append(btg * st["wst_r"][hh:hh + 1, :])
            rstack.append(st["xs_b"][:, gl] * head_masks[r])
            cdec_x = cdec_x + head_masks_f[r] * st["cdec"][hh:hh + 1, :]
        lhs = jnp.concatenate([jnp.concatenate(lms, axis=1), jnp.concatenate(bts, axis=1)], axis=0)
        res = _dot(lhs, jnp.concatenate(rstack, axis=0))
        st["ys"].append(res[0:CHUNK] + y_off)
        ht_s[d][g][...] = htg * cdec_x + res[CHUNK:2 * CHUNK]

    def scan_pair(cf, cb):
        sts = [scan_sums(cf, 0), scan_sums(cb, 1)]
        for st in sts:
            scan_spread(st)
        for g in range(SSD_GROUPS):
            for st in sts:
                scan_group(st, g)
        for st in sts:
            st["y"] = jnp.concatenate(st["ys"], axis=1)
        return sts

    for d in range(2):
        for g in range(SSD_GROUPS):
            ht_s[d][g][...] = h0_ref[0, d, g]
    cw = cw_ref[0]
    rows1 = lax.broadcasted_iota(I32, (CHUNK, 1), 0)

    def conv_body(c, carry):
        r0 = pl.multiple_of(c * CHUNK, CHUNK)
        x = xbc_ref[0, pl.ds(r0, CHUNK), :].astype(F32)
        rp0 = pl.multiple_of(jnp.maximum(r0 - 16, 0), 16)
        rn0 = pl.multiple_of(jnp.minimum(r0 + CHUNK, n - 16), 16)
        prev = xbc_ref[0, pl.ds(rp0, 16), :].astype(F32)[15:16, :]
        nxt = xbc_ref[0, pl.ds(rn0, 16), :].astype(F32)[0:1, :]
        prev = jnp.where(c > 0, prev, 0.0)
        nxt = jnp.where(c < nc - 1, nxt, 0.0)
        x_prev = jnp.where(rows1 == 0, prev, pltpu.roll(x, 1, 0))
        x_next = jnp.where(rows1 == CHUNK - 1, nxt, pltpu.roll(x, CHUNK - 1, 0))
        xc = _silu(cb_ref[0] + x_prev * cw[0:1] + x * cw[1:2] + x_next * cw[2:3])
        bm = xc[:, SSD_INNER:SSD_INNER + 256]
        bt = jnp.concatenate([bm[:, 0:D_STATE].T, bm[:, D_STATE:2 * D_STATE].T], axis=0).astype(BF16)
        xs_s[pl.ds(r0, CHUNK), :] = xc[:, 0:SSD_INNER]
        c_s[pl.ds(r0, CHUNK), :] = xc[:, SSD_INNER + 256:SSD_INNER + 512].astype(BF16)
        bt_s[:, pl.ds(r0, CHUNK)] = bt
        return carry

    lax.fori_loop(0, nc, conv_body, 0)

    def finish_chunk(st):
        r0, d = st["r0"], st["d"]
        y = yacc_s[1 - d][pl.ds(r0, CHUNK), :] + st["y"] + dsk_ref[0] * st["xs"]
        zz = z_ref[0, pl.ds(r0, CHUNK), :].astype(F32)
        g = y * _silu(zz)
        outs = [_rms_rows(g[:, gi * 256:(gi + 1) * 256]) for gi in range(SSD_GROUPS)]
        y_ref[0, pl.ds(r0, CHUNK), :] = (jnp.concatenate(outs, axis=1) * nw_ref[0]).astype(BF16)

    half = nc // 2

    def outer_body(i, carry):
        for st in scan_pair(i, nc - 1 - i):
            yacc_s[st["d"]][pl.ds(st["r0"], CHUNK), :] = st["y"]
        return carry

    def inner_body(i, carry):
        for st in scan_pair(half + i, half - 1 - i):
            finish_chunk(st)
        return carry

    lax.fori_loop(0, half, outer_body, 0)
    lax.fori_loop(0, half, inner_body, 0)
    for d in range(2):
        for g in range(SSD_GROUPS):
            hfin_ref[0, d, g] = ht_s[d][g][...]


def _ssd_mixer(xbc, z, dtt, h0, p, layer):
    b, n, _ = xbc.shape
    lsel3 = lambda bi: (layer, 0, 0)
    st_spec = pl.BlockSpec((1, 2, SSD_GROUPS, D_STATE, 256), lambda bi: (bi, 0, 0, 0, 0))
    return pl.pallas_call(
        functools.partial(_ssd_kernel, n=n),
        grid=(b,),
        in_specs=[
            pl.BlockSpec((1, n, CONV_DIM), lambda bi: (bi, 0, 0)),
            pl.BlockSpec((1, n, SSD_INNER), lambda bi: (bi, 0, 0)),
            pl.BlockSpec((1, N_DT, n), lambda bi: (bi, 0, 0)),
            pl.BlockSpec((1, 3, CONV_DIM), lsel3),
            pl.BlockSpec((1, 1, CONV_DIM), lsel3),
            pl.BlockSpec((1, N_DT, 1), lsel3),
            pl.BlockSpec((1, N_DT, 1), lsel3),
            pl.BlockSpec((1, 1, SSD_INNER), lsel3),
            pl.BlockSpec((1, 1, SSD_INNER), lsel3),
            pl.BlockSpec((CHUNK, SSD_HEADS * CHUNK + SSD_INNER), lambda bi: (0, 0)),
            st_spec,
        ],
        out_specs=[pl.BlockSpec((1, n, SSD_INNER), lambda bi: (bi, 0, 0)), st_spec],
        out_shape=[jax.ShapeDtypeStruct((b, n, SSD_INNER), BF16),
                   jax.ShapeDtypeStruct((b, 2, SSD_GROUPS, D_STATE, 256), F32)],
        scratch_shapes=[
            pltpu.VMEM((n, SSD_INNER), F32),
            pltpu.VMEM((n, 2 * D_STATE), BF16),
            pltpu.VMEM((2 * D_STATE, n), BF16),
            pltpu.VMEM((n, SSD_INNER), F32),
            pltpu.VMEM((n, SSD_INNER), F32),
        ] + [pltpu.VMEM((D_STATE, 256), F32)] * (2 * SSD_GROUPS),
        compiler_params=_cparams(("parallel",)),
        name="ssd_mixer",
    )(xbc, z, dtt, p["conv_w"], p["conv_b"], p["dt_bias_t"], p["a_log_t"], p["d_skip_x"], p["ssd_norm_w"],
      p["spread"], h0)


def _outproj_kernel(a_ref, s_ref, x_ref, mod_ref, wa_ref, ws_ref, nw_ref, wr_ref, xo_ref, h2_ref, aff_ref):
    mix = _dot(a_ref[0], wa_ref[0]) + _dot(s_ref[0], ws_ref[0])
    x = x_ref[0] + mod_ref[0, 0, 2:3, :] * mix
    xo_ref[0] = x
    h2 = (_rms_rows(x) * nw_ref[0] * (1.0 + mod_ref[0, 0, 4:5, :]) + mod_ref[0, 0, 3:4, :]).astype(BF16)
    h2_ref[0] = h2
    lg = _dot(h2, wr_ref[0])
    lgt = lg.T[0:N_EXPERTS, :]
    e = jnp.exp(lgt - jnp.max(lgt, axis=0, keepdims=True))
    aff_ref[0] = e / jnp.sum(e, axis=0, keepdims=True)


def _out_projection(attn, ssd, x, mod, mod_rows, layer, p, tm, group):
    b, n, d = x.shape
    nt = n // tm
    if mod_rows == 1:
        mod_map = lambda bi, i: (layer, mod.shape[1] - 1, 0, 0)
    else:
        mod_map = lambda bi, i: (layer, bi, 0, 0)
    lsel = lambda bi, i: (layer, 0, 0)
    row = lambda w: pl.BlockSpec((1, tm, w), lambda bi, i: (bi, i, 0))
    return pl.pallas_call(
        _outproj_kernel,
        grid=(b, nt),
        in_specs=[row(ATTN_WIDTH), row(SSD_INNER), row(d),
                  pl.BlockSpec((1, 1, N_MOD, d), mod_map),
                  pl.BlockSpec((1, ATTN_WIDTH, d), lsel),
                  pl.BlockSpec((1, SSD_INNER, d), lsel),
                  pl.BlockSpec((1, 1, d), lsel),
                  pl.BlockSpec((1, d, LANES), lsel)],
        out_specs=[row(d), row(d),
                   pl.BlockSpec((1, N_EXPERTS, tm), lambda bi, i: (bi // group, 0, (bi % group) * nt + i))],
        out_shape=[jax.ShapeDtypeStruct((b, n, d), F32),
                   jax.ShapeDtypeStruct((b, n, d), BF16),
                   jax.ShapeDtypeStruct((b // group, N_EXPERTS, group * n), F32)],
        compiler_params=_cparams(("parallel", "arbitrary")),
        name="out_projection",
    )(attn, ssd, x, mod, p["w_out_a"], p["w_out_s"], p["norm2_w"], p["w_router"])


def _route_kernel(aff_ref, slot_ref, *, n, group, cap):
    bits_all = pltpu.bitcast(aff_ref[0], I32)
    li = lax.broadcasted_iota(I32, (LANES, LANES), 0)
    lj = lax.broadcasted_iota(I32, (LANES, LANES), 1)
    strict_up = jnp.where(li < lj, 1.0, 0.0).astype(BF16)
    fcap = float(cap)

    def count(m):
        return jnp.sum(jnp.where(m, 1.0, 0.0), axis=-1, keepdims=True)

    def excl_prefix(m):
        total = jnp.zeros((N_EXPERTS, 1), F32)
        outs = []
        for j in range(n // LANES):
            blk = jnp.where(m[:, j * LANES:(j + 1) * LANES], 1.0, 0.0)
            outs.append(_dot(blk.astype(BF16), strict_up) + total)
            total = total + jnp.sum(blk, axis=-1, keepdims=True)
        return jnp.concatenate(outs, axis=1)

    for s in range(group):
        bits = bits_all[:, s * n:(s + 1) * n]

        def bisect(i, lo):
            cand = lo | (jnp.int32(1) << (30 - i))
            return jnp.where(count(bits >= cand) >= fcap, cand, lo)

        thr = lax.fori_loop(0, 31, bisect, jnp.zeros((N_EXPERTS, 1), I32))
        gt = bits > thr
        eq = bits == thr
        need = fcap - count(gt)
        sel = gt | (eq & (excl_prefix(eq) < need))
        pos = excl_prefix(sel)
        slot_ref[0, :, s * n:(s + 1) * n] = jnp.where(sel, pos.astype(I32) + s * cap, -1)


def _route(aff, n, group, cap):
    ng, _, t = aff.shape
    return pl.pallas_call(
        functools.partial(_route_kernel, n=n, group=group, cap=cap),
        grid=(ng,),
        in_specs=[pl.BlockSpec((1, N_EXPERTS, t), lambda g: (g, 0, 0))],
        out_specs=pl.BlockSpec((1, N_EXPERTS, t), lambda g: (g, 0, 0)),
        out_shape=jax.ShapeDtypeStruct((ng, N_EXPERTS, t), I32),
        compiler_params=_cparams(("parallel",)),
        name="ec_route",
    )(aff)


def _gather_kernel(slot_ref, aff_ref, h_ref, xe_ref, gate_ref, *, slots):
    t = h_ref.shape[1]
    ne = slot_ref.shape[2]
    sio = lax.broadcasted_iota(I32, (slots, t), 0)
    pm, gates = [], []
    for j in range(ne):
        hit = sio == slot_ref[0, 0, j:j + 1, :]
        pm.append(jnp.where(hit, 1.0, 0.0).astype(BF16))
        gates.append(jnp.sum(jnp.where(hit, aff_ref[0, 0, j:j + 1, :], 0.0), axis=-1, keepdims=True))
    xe = _dot(jnp.concatenate(pm, axis=0), h_ref[0])
    for j in range(ne):
        xe_ref[0, j] = xe[j * slots:(j + 1) * slots].astype(BF16)
        gate_ref[0, j] = jnp.broadcast_to(gates[j], (slots, LANES))


def _ffn_kernel(xe_ref, gate_ref, wg_ref, wu_ref, wd_ref, ye_ref):
    nb, _, s, d = xe_ref.shape
    xe = xe_ref[...].reshape(nb * s, d)
    gate = gate_ref[...].reshape(nb * s, LANES)[:, 0:1]
    hid = (_silu(_dot(xe, wg_ref[0, 0])) * _dot(xe, wu_ref[0, 0])).astype(BF16)
    ye = (_dot(hid, wd_ref[0, 0]) * gate).astype(BF16)
    ye_ref[...] = ye.reshape(nb, 1, s, d)


def _scatter_kernel(slot_ref, ye_ref, x_ref, mod_ref, o_ref, *, slots, chunk):
    tt = x_ref.shape[1]
    ne = slot_ref.shape[1]
    sio = lax.broadcasted_iota(I32, (slots, tt), 0)
    acc = jnp.zeros(x_ref.shape[1:], F32)
    for c0 in range(0, ne, chunk):
        pm = [jnp.where(sio == slot_ref[0, j:j + 1, :], 1.0, 0.0).astype(BF16) for j in range(c0, c0 + chunk)]
        ye = ye_ref[0, c0:c0 + chunk].reshape(chunk * slots, ye_ref.shape[3])
        acc = acc + _dot_tn(jnp.concatenate(pm, axis=0), ye)
    o_ref[0] = x_ref[0] + mod_ref[0, 0, 5:6, :] * acc


def _moe_residual(slot, aff, h2, x_mid, mod, mod_rows, p, layer, group):
    b, n, d = h2.shape
    ng, t = b // group, group * n
    cap = EC_CAPACITY * n // N_EXPERTS
    slots = group * cap
    ech = 4
    xe, gate = pl.pallas_call(
        functools.partial(_gather_kernel, slots=slots),
        grid=(ng, N_EXPERTS // ech),
        in_specs=[
            pl.BlockSpec((1, 1, ech, t), lambda g, c: (g, c, 0, 0)),
            pl.BlockSpec((1, 1, ech, t), lambda g, c: (g, c, 0, 0)),
            pl.BlockSpec((1, t, d), lambda g, c: (g, 0, 0)),
        ],
        out_specs=[pl.BlockSpec((1, ech, slots, d), lambda g, c: (g, c, 0, 0)),
                   pl.BlockSpec((1, ech, slots, LANES), lambda g, c: (g, c, 0, 0))],
        out_shape=[jax.ShapeDtypeStruct((ng, N_EXPERTS, slots, d), BF16),
                   jax.ShapeDtypeStruct((ng, N_EXPERTS, slots, LANES), F32)],
        compiler_params=_cparams(("parallel", "arbitrary")),
        name="ec_gather",
    )(slot.reshape(ng, N_EXPERTS // ech, ech, t), aff.reshape(ng, N_EXPERTS // ech, ech, t),
      h2.reshape(ng, t, d))

    nb = min(ng, 2)
    wsel = lambda e, g: (layer, e, 0, 0)
    ye = pl.pallas_call(
        _ffn_kernel,
        grid=(N_EXPERTS, ng // nb),
        in_specs=[
            pl.BlockSpec((nb, 1, slots, d), lambda e, g: (g, e, 0, 0)),
            pl.BlockSpec((nb, 1, slots, LANES), lambda e, g: (g, e, 0, 0)),
            pl.BlockSpec((1, 1, d, D_EXPERT), wsel),
            pl.BlockSpec((1, 1, d, D_EXPERT), wsel),
            pl.BlockSpec((1, 1, D_EXPERT, d), wsel),
        ],
        out_specs=pl.BlockSpec((nb, 1, slots, d), lambda e, g: (g, e, 0, 0)),
        out_shape=jax.ShapeDtypeStruct((ng, N_EXPERTS, slots, d), BF16),
        compiler_params=_cparams(("parallel", "arbitrary")),
        name="ec_ffn",
    )(xe, gate, p["w_gate"], p["w_up"], p["w_down"])

    tt = min(t, 512)
    nt = t // tt
    if mod_rows == 1:
        mod_map = lambda g, i: (layer, mod.shape[1] - 1, 0, 0)
    else:
        mod_map = lambda g, i: (layer, g, 0, 0)
    out = pl.pallas_call(
        functools.partial(_scatter_kernel, slots=slots, chunk=4),
        grid=(ng, nt),
        in_specs=[
            pl.BlockSpec((1, N_EXPERTS, tt), lambda g, i: (g, 0, i)),
            pl.BlockSpec((1, N_EXPERTS, slots, d), lambda g, i: (g, 0, 0, 0)),
            pl.BlockSpec((1, tt, d), lambda g, i: (g, i, 0)),
            pl.BlockSpec((1, 1, N_MOD, d), mod_map),
        ],
        out_specs=pl.BlockSpec((1, tt, d), lambda g, i: (g, i, 0)),
        out_shape=jax.ShapeDtypeStruct((ng, t, d), F32),
        compiler_params=_cparams(("parallel", "arbitrary")),
        name="ec_scatter",
    )(slot, ye, x_mid.reshape(ng, t, d), mod)
    return out.reshape(b, n, d)


def _rope_tables(n):
    rows = n // GRID_W
    pos = np.stack([np.repeat(np.arange(rows), GRID_W), np.tile(np.arange(GRID_W), rows)], axis=-1)
    n_freq = HEAD_DIM // 4
    inv = jnp.asarray(ROPE_BASE, F32) ** (-jnp.arange(n_freq, dtype=F32) / n_freq)
    ang = jnp.asarray(pos, F32)[..., None] * inv
    cos, sin = jnp.cos(ang), jnp.sin(ang)
    zero = jnp.zeros_like(sin)
    cos64 = jnp.concatenate([cos, cos], axis=-1).reshape(n, HEAD_DIM)
    sa64 = jnp.concatenate([-sin, zero], axis=-1).reshape(n, HEAD_DIM)
    sb64 = jnp.concatenate([zero, sin], axis=-1).reshape(n, HEAD_DIM)
    two = lambda t: jnp.concatenate([t, t], axis=-1)
    return two(cos64), two(sa64), two(sb64)


def _prepare(w_in, norm1_w, norm2_w, q_norm_w, k_norm_w, lambda_q1, lambda_k1, lambda_q2, lambda_k2, subln_w,
             conv_w, conv_b, dt_bias_f, dt_bias_b, a_log_f, a_log_b, d_skip, ssd_norm_w, w_out, w_router,
             w_gate, w_up, w_down):
    nl = w_in.shape[0]
    gi = np.arange(MXU_WIDTH) // HEAD_DIM
    dt_bias = jnp.concatenate([dt_bias_f, dt_bias_b], axis=-1)
    a_log = jnp.concatenate([a_log_f, a_log_b], axis=-1)
    tile2 = lambda w: jnp.concatenate([w, w], axis=-1).reshape(nl, 1, LANES)
    qk_max = jnp.max(jnp.abs(q_norm_w), axis=-1) * jnp.max(jnp.abs(k_norm_w), axis=-1)
    kk = np.arange(CHUNK)[:, None]
    jj = np.arange(SSD_HEADS * CHUNK + SSD_INNER)[None, :]
    to_cs = (kk < 3 * SSD_HEADS) & (jj < SSD_HEADS * CHUNK) & (kk % SSD_HEADS == jj // CHUNK)
    to_ecs = ((kk >= 3 * SSD_HEADS) & (kk < 6 * SSD_HEADS) & (jj >= SSD_HEADS * CHUNK)
              & (kk % SSD_HEADS == (jj - SSD_HEADS * CHUNK) // SSD_HEADDIM))
    return {
        "w_main": w_in[:, :, :P_MAIN].astype(BF16),
        "w_dtt": jnp.swapaxes(w_in[:, :, P_MAIN:], 1, 2).astype(BF16),
        "gmat": jnp.asarray(gi[:, None] == gi[None, :], BF16),
        "norm1_w": norm1_w.reshape(nl, 1, D_MODEL),
        "norm2_w": norm2_w.reshape(nl, 1, D_MODEL),
        "q_norm_w": tile2(q_norm_w),
        "k_norm_w": tile2(k_norm_w),
        "lam": jnp.stack([lambda_q1, lambda_k1, lambda_q2, lambda_k2], axis=1),
        "subln_w": subln_w.reshape(nl, 1, ATTN_VDIM),
        "logit_bound": LOGIT_MARGIN * HEAD_DIM ** 0.5 * qk_max,
        "conv_w": conv_w,
        "conv_b": conv_b.reshape(nl, 1, CONV_DIM),
        "dt_bias_t": dt_bias.reshape(nl, N_DT, 1),
        "a_log_t": a_log.reshape(nl, N_DT, 1),
        "d_skip_x": jnp.repeat(d_skip, SSD_HEADDIM, axis=-1).reshape(nl, 1, SSD_INNER),
        "ssd_norm_w": ssd_norm_w.reshape(nl, 1, SSD_INNER),
        "spread": jnp.asarray(to_cs | to_ecs, BF16),
        "w_out_a": w_out[:, :ATTN_WIDTH, :].astype(BF16),
        "w_out_s": w_out[:, ATTN_WIDTH:, :].astype(BF16),
        "w_router": jnp.pad(w_router, ((0, 0), (0, 0), (0, LANES - N_EXPERTS))).astype(BF16),
        "w_gate": w_gate.astype(BF16),
        "w_up": w_up.astype(BF16),
        "w_down": w_down.astype(BF16),
    }


def _ffn(h2, aff, x_mid, mod, mod_rows, layer, p, group):
    n = h2.shape[1]
    slot = _route(aff, n, group, EC_CAPACITY * n // N_EXPERTS)
    return _moe_residual(slot, aff, h2, x_mid, mod, mod_rows, p, layer, group)


def kernel(x, c, ctx, c_ctx, w_mod, b_mod, norm1_w, norm2_w, w_in, q_norm_w, k_norm_w, lambda_q1, lambda_k1,
           lambda_q2, lambda_k2, subln_w, conv_w, conv_b, dt_bias_f, dt_bias_b, a_log_f, a_log_b, d_skip,
           ssd_norm_w, w_out, w_router, w_gate, w_up, w_down):
    b, n, d = x.shape
    m = ctx.shape[1]
    nl = w_in.shape[0]
    assert d == D_MODEL and n % GRID_W == 0 and n % (2 * CHUNK) == 0 and m % (2 * CHUNK) == 0 and n % m == 0
    assert MOE_TOKENS % n == 0 and MOE_TOKENS % m == 0
    g_lat, g_ctx = MOE_TOKENS // n, MOE_TOKENS // m
    assert b % g_lat == 0 and b % g_ctx == 0
    tm_lat, tm_ctx = min(n, 512), min(m, 256)
    tq = min(n, 512)

    p = _prepare(w_in, norm1_w, norm2_w, q_norm_w, k_norm_w, lambda_q1, lambda_k1, lambda_q2, lambda_k2,
                 subln_w, conv_w, conv_b, dt_bias_f, dt_bias_b, a_log_f, a_log_b, d_skip, ssd_norm_w, w_out,
                 w_router, w_gate, w_up, w_down)
    rope = _rope_tables(n)
    mod = _modulation(jnp.concatenate([c, c_ctx[None, :]], axis=0), w_mod, b_mod)
    h_zero = jnp.zeros((b, 2, SSD_GROUPS, D_STATE, 256), F32)

    x_lat, x_ctx = x, ctx
    for l in range(nl):
        update_ctx = l < nl - 1
        lam_init = 0.8 - 0.6 * math.exp(-0.3 * l)
        q, k_all, v_all, z, xbc, dtt = _in_projection(x_lat, mod, b, l, p, tm_lat, n + m, rope_tabs=rope)
        qc, k_all, v_all, zc, xbcc, dttc = _in_projection(x_ctx, mod, 1, l, p, tm_ctx, n + m, kv=(k_all, v_all))
        ssd_c, h_ctx = _ssd_mixer(xbcc, zc, dttc, h_zero, p, l)
        ssd, _ = _ssd_mixer(xbc, z, dtt, h_ctx, p, l)
        attn = _diff_attention(q, k_all, v_all, p, l, lam_init, tq, n + m, 0)
        x_mid, h2, aff = _out_projection(attn, ssd, x_lat, mod, b, l, p, tm_lat, g_lat)
        x_lat = _ffn(h2, aff, x_mid, mod, b, l, p, g_lat)
        if update_ctx:
            attn_c = _diff_attention(qc, k_all, v_all, p, l, lam_init, min(m, 256), m, n // m)
            xc_mid, hc2, aff_c = _out_projection(attn_c, ssd_c, x_ctx, mod, 1, l, p, tm_ctx, g_ctx)
            x_ctx = _ffn(hc2, aff_c, xc_mid, mod, 1, l, p, g_ctx)
    return x_lat
```

```python
import functools
import math

import jax
import jax.numpy as jnp
import numpy as np
from jax import lax
from jax.experimental import pallas as pl
from jax.experimental.pallas import tpu as pltpu

F32 = jnp.float32
BF16 = jnp.bfloat16
I32 = jnp.int32

D_MODEL = 1024
DEPTH = 4
GRID_W = 64
N_MOD = 6
EPS = 1e-6
ATTN_HEADS = 4
HEAD_DIM = 64
ATTN_VDIM = 128
Q_WIDTH = 512
ATTN_WIDTH = 512
ROPE_BASE = 10000.0
SSD_HEADDIM = 64
SSD_GROUPS = 2
SSD_HPG = 4
SSD_HEADS = 8
SSD_INNER = 512
D_STATE = 128
CHUNK = 128
CONV_DIM = 1024
P_MAIN = 3072
N_DT = 2 * SSD_HEADS
N_EXPERTS = 16
EC_CAPACITY = 2
D_EXPERT = 1408
LANES = 128
MXU_WIDTH = 256
MOE_TOKENS = 2048
QK_SCALE = HEAD_DIM ** -0.5 * math.log2(math.e)
LOGIT_MARGIN = 1.03
ATTN_SAFE_LOGIT = 30.0
VMEM_LIMIT = 56 * 1024 * 1024


def _cparams(sem):
    return pltpu.CompilerParams(dimension_semantics=sem, vmem_limit_bytes=VMEM_LIMIT)


def _sigmoid(x):
    return 0.5 * jnp.tanh(0.5 * x) + 0.5


def _silu(x):
    return x * _sigmoid(x)


def _softplus(x):
    return jnp.maximum(x, 0.0) + jnp.log1p(jnp.exp(-jnp.abs(x)))


def _dot(a, b):
    return jnp.dot(a, b, preferred_element_type=F32)


def _dot_nt(a, b):
    return lax.dot_general(a, b, (((1,), (1,)), ((), ())), preferred_element_type=F32)


def _dot_tn(a, b):
    return lax.dot_general(a, b, (((0,), (0,)), ((), ())), preferred_element_type=F32)


def _split3(x):
    hi = x.astype(BF16)
    r = x - hi.astype(F32)
    mid = r.astype(BF16)
    lo = (r - mid.astype(F32)).astype(BF16)
    return hi, mid, lo


def _mod_kernel(cc_ref, w_ref, b_ref, o_ref):
    s = _silu(cc_ref[...]).astype(BF16)
    o_ref[0] = _dot(s, w_ref[0].astype(BF16)) + b_ref[0]


def _modulation(cc, w_mod, b_mod):
    nl, d, nm = w_mod.shape
    r = cc.shape[0]
    tn = 1536
    out = pl.pallas_call(
        _mod_kernel,
        grid=(nl, nm // tn),
        in_specs=[
            pl.BlockSpec((r, d), lambda l, j: (0, 0)),
            pl.BlockSpec((1, d, tn), lambda l, j: (l, 0, j)),
            pl.BlockSpec((1, 1, tn), lambda l, j: (l, 0, j)),
        ],
        out_specs=pl.BlockSpec((1, r, tn), lambda l, j: (l, 0, j)),
        out_shape=jax.ShapeDtypeStruct((nl, r, nm), F32),
        compiler_params=_cparams(("arbitrary", "arbitrary")),
        name="modulation",
    )(cc, w_mod, b_mod.reshape(nl, 1, nm))
    return out.reshape(nl, r, N_MOD, d)


def _rms_rows(x):
    return x * lax.rsqrt(jnp.mean(x * x, axis=-1, keepdims=True) + EPS)


def _head_norm(t, gmat, w128, rope_tabs, scale):
    t2 = (t * t).astype(BF16)
    gw = gmat.shape[0]
    ss = jnp.concatenate([_dot(t2[:, j:j + gw], gmat) for j in range(0, Q_WIDTH, gw)], axis=1)
    y = t * lax.rsqrt(ss * (1.0 / HEAD_DIM) + EPS)
    outs = []
    for j in range(Q_WIDTH // LANES):
        yj = y[:, j * LANES:(j + 1) * LANES] * w128
        if rope_tabs is not None:
            cos, sa, sb = rope_tabs
            up = pltpu.roll(yj, LANES - 16, 1)
            dn = pltpu.roll(yj, 16, 1)
            yj = yj * cos + up * sa + dn * sb
        outs.append(yj * scale)
    return jnp.concatenate(outs, axis=1).astype(BF16)


def _inproj_kernel(*refs, rope):
    if rope:
        (x_ref, mod_ref, nw_ref, w_ref, wdtt_ref, g_ref, qw_ref, kw_ref,
         cos_ref, sa_ref, sb_ref, q_o, k_o, v_o, z_o, xbc_o, dtt_o) = refs
        tabs = (cos_ref[...], sa_ref[...], sb_ref[...])
    else:
        (x_ref, mod_ref, nw_ref, w_ref, wdtt_ref, g_ref, qw_ref, kw_ref, _, _,
         q_o, k_o, v_o, z_o, xbc_o, dtt_o) = refs
        tabs = None
    x = x_ref[0]
    sh = mod_ref[0, 0, 0:1, :]
    sc = mod_ref[0, 0, 1:2, :]
    h = (_rms_rows(x) * nw_ref[0] * (1.0 + sc) + sh).astype(BF16)
    gmat = g_ref[...]
    pq = _dot(h, w_ref[0, :, 0:512])
    pk = _dot(h, w_ref[0, :, 512:1024])
    q_o[0] = _head_norm(pq, gmat, qw_ref[0], tabs, QK_SCALE)
    pv = _dot(h, w_ref[0, :, 1024:1536])
    k_o[0] = _head_norm(pk, gmat, kw_ref[0], tabs, 1.0)
    pz = _dot(h, w_ref[0, :, 1536:2048])
    v_o[0] = pv.astype(BF16)
    px = _dot(h, w_ref[0, :, 2048:3072])
    z_o[0] = pz.astype(BF16)
    dtt_o[0] = _dot_nt(wdtt_ref[0], h)
    xbc_o[0] = px.astype(BF16)


def _in_projection(x, mod, mod_rows, layer, p, tm, kv_rows, rope_tabs=None, kv=None):
    b, n, d = x.shape
    rope = rope_tabs is not None
    assert rope != (kv is not None)
    kv_blk0 = 0 if rope else (kv_rows - n) // tm
    if mod_rows == 1:
        mod_map = lambda bi, i: (layer, mod.shape[1] - 1, 0, 0)
    else:
        mod_map = lambda bi, i: (layer, bi, 0, 0)
    lsel = lambda bi, i: (layer, 0, 0)
    in_specs = [
        pl.BlockSpec((1, tm, d), lambda bi, i: (bi, i, 0)),
        pl.BlockSpec((1, 1, N_MOD, d), mod_map),
        pl.BlockSpec((1, 1, d), lsel),
        pl.BlockSpec((1, d, P_MAIN), lsel),
        pl.BlockSpec((1, N_DT, d), lsel),
        pl.BlockSpec((MXU_WIDTH, MXU_WIDTH), lambda bi, i: (0, 0)),
        pl.BlockSpec((1, 1, LANES), lsel),
        pl.BlockSpec((1, 1, LANES), lsel),
    ]
    args = [x, mod, p["norm1_w"], p["w_main"], p["w_dtt"], p["gmat"], p["q_norm_w"], p["k_norm_w"]]
    if rope:
        in_specs += [pl.BlockSpec((tm, LANES), lambda bi, i: (i, 0))] * 3
        args += list(rope_tabs)
        aliases = {}
    else:
        in_specs += [pl.BlockSpec(memory_space=pl.ANY)] * 2
        aliases = {len(args): 1, len(args) + 1: 2}
        args += list(kv)
    row = lambda w: pl.BlockSpec((1, tm, w), lambda bi, i: (bi, i, 0))
    kv_row = pl.BlockSpec((1, tm, 512), lambda bi, i: (bi, kv_blk0 + i, 0))
    out_specs = [row(512), kv_row, kv_row, row(512), row(1024),
                 pl.BlockSpec((1, N_DT, tm), lambda bi, i: (bi, 0, i))]
    out_shape = [jax.ShapeDtypeStruct((b, n, 512), BF16),
                 jax.ShapeDtypeStruct((b, kv_rows, 512), BF16),
                 jax.ShapeDtypeStruct((b, kv_rows, 512), BF16),
                 jax.ShapeDtypeStruct((b, n, 512), BF16),
                 jax.ShapeDtypeStruct((b, n, 1024), BF16),
                 jax.ShapeDtypeStruct((b, N_DT, n), F32)]
    return pl.pallas_call(
        functools.partial(_inproj_kernel, rope=rope),
        grid=(b, n // tm),
        in_specs=in_specs,
        out_specs=out_specs,
        out_shape=out_shape,
        input_output_aliases=aliases,
        compiler_params=_cparams(("parallel", "arbitrary")),
        name="in_projection_rope" if rope else "in_projection",
    )(*args)


def _attn_kernel(bound_ref, q_ref, k_ref, v_ref, lam_ref, sw_ref, o_ref, *, layer, lam_init):
    tq = q_ref.shape[1]
    lp = lam_ref[0]
    lam = (jnp.exp(jnp.sum(lp[0:1] * lp[1:2], axis=-1, keepdims=True))
           - jnp.exp(jnp.sum(lp[2:3] * lp[3:4], axis=-1, keepdims=True)) + lam_init)
    lane = lax.broadcasted_iota(I32, (1, LANES), 1)
    m_lo = jnp.where(lane < HEAD_DIM, 1.0, 0.0).astype(BF16)
    m_hi = jnp.where(lane < HEAD_DIM, 0.0, 1.0).astype(BF16)
    sw = sw_ref[0]

    def heads(shift_by_max):
        for h in range(ATTN_HEADS):
            cols = slice(h * LANES, (h + 1) * LANES)
            qh = q_ref[0, :, cols]
            qs = jnp.concatenate([qh * m_lo, qh * m_hi], axis=0)
            s = _dot_nt(qs, k_ref[0, :, cols])
            if shift_by_max:
                s = s - jnp.max(s, axis=-1, keepdims=True)
            p = jnp.exp2(s)
            r = 1.0 / jnp.sum(p, axis=-1, keepdims=True)
            pv = _dot(p.astype(BF16), v_ref[0, :, cols]) * r
            o = pv[0:tq] - lam * pv[tq:2 * tq]
            o = _rms_rows(o) * sw * (1.0 - lam_init)
            o_ref[0, :, cols] = o.astype(BF16)

    small = bound_ref[layer] < ATTN_SAFE_LOGIT

    @pl.when(small)
    def _():
        heads(False)

    @pl.when(jnp.logical_not(small))
    def _():
        heads(True)


def _diff_attention(q, k_all, v_all, p, layer, lam_init, tq, nk, kv_block):
    b, n, _ = q.shape
    return pl.pallas_call(
        functools.partial(_attn_kernel, layer=layer, lam_init=lam_init),
        grid=(b, n // tq),
        in_specs=[
            pl.BlockSpec(memory_space=pltpu.SMEM),
            pl.BlockSpec((1, tq, Q_WIDTH), lambda bi, i: (bi, i, 0)),
            pl.BlockSpec((1, nk, Q_WIDTH), lambda bi, i: (bi, kv_block, 0)),
            pl.BlockSpec((1, nk, ATTN_WIDTH), lambda bi, i: (bi, kv_block, 0)),
            pl.BlockSpec((1, 4, HEAD_DIM), lambda bi, i: (layer, 0, 0)),
            pl.BlockSpec((1, 1, ATTN_VDIM), lambda bi, i: (layer, 0, 0)),
        ],
        out_specs=pl.BlockSpec((1, tq, ATTN_WIDTH), lambda bi, i: (bi, i, 0)),
        out_shape=jax.ShapeDtypeStruct((b, n, ATTN_WIDTH), BF16),
        compiler_params=_cparams(("parallel", "arbitrary")),
        name="diff_attention",
    )(p["logit_bound"], q, k_all, v_all, p["lam"], p["subln_w"])


def _ssd_kernel(xbc_ref, z_ref, dtt_ref, cw_ref, cb_ref, dtbt_ref, alt_ref, dsk_ref, nw_ref, sp_ref, h0_ref,
                y_ref, hfin_ref, xs_s, c_s, bt_s, yf_s, yb_s, hf0_s, hf1_s, hb0_s, hb1_s, *, n):
    nc = n // CHUNK
    ht_s = ((hf0_s, hf1_s), (hb0_s, hb1_s))
    yacc_s = (yf_s, yb_s)
    row = lax.broadcasted_iota(I32, (CHUNK, CHUNK), 0)
    col = lax.broadcasted_iota(I32, (CHUNK, CHUNK), 1)
    tri_lo = col <= row
    tri_up = col >= row
    t_lo = jnp.where(tri_lo, 1.0, 0.0).astype(BF16)
    t_up = jnp.where(tri_up, 1.0, 0.0).astype(BF16)
    a_col = -jnp.exp(alt_ref[0])
    lane_head = lax.broadcasted_iota(I32, (1, SSD_INNER // 2), 1) // SSD_HEADDIM
    head_masks_f = [jnp.where(lane_head == r, 1.0, 0.0) for r in range(SSD_HPG)]
    head_masks = [m.astype(BF16) for m in head_masks_f]
    zero_rows = jnp.zeros((CHUNK - 6 * SSD_HEADS, CHUNK), F32)
    spread_mat = sp_ref[...]


    def scan_sums(c, d):
        r0 = pl.multiple_of(c * CHUNK, CHUNK)
        hs = slice(d * SSD_HEADS, (d + 1) * SSD_HEADS)
        dt_r = _softplus(dtt_ref[0, hs, pl.ds(r0, CHUNK)] + dtbt_ref[0, hs, :])
        hi, mid, lo = _split3(dt_r * a_col[hs])
        da3 = jnp.concatenate([hi.astype(F32), mid.astype(F32), lo.astype(F32)], axis=0).astype(BF16)
        cs3 = _dot(da3, t_up if d == 0 else t_lo)
        cs_r = cs3[0:8] + cs3[8:16] + cs3[16:24]
        edge = CHUNK - 1 if d == 0 else 0
        cs_edge = cs_r[:, edge:edge + 1]
        return dict(c=c, d=d, r0=r0, dt_r=dt_r, cs_r=cs_r,
                    wst_r=(dt_r * jnp.exp(cs_edge - cs_r)).astype(BF16),
                    cdec=jnp.exp(cs_edge))

    def scan_spread(st):
        cs_r, r0 = st["cs_r"], st["r0"]
        pieces = [t.astype(F32) for t in _split3(cs_r) + _split3(jnp.exp(cs_r))]
        colf = jnp.concatenate(pieces + [zero_rows], axis=0).T.astype(BF16)
        st["spread"] = _dot(colf, spread_mat)
        st["xs"] = xs_s[pl.ds(r0, CHUNK), :]
        st["xs_b"] = st["xs"].astype(BF16)
        st["cm"] = c_s[pl.ds(r0, CHUNK), :]
        st["bt"] = bt_s[:, pl.ds(r0, CHUNK)]
        st["ys"] = []

    def scan_group(st, g):
        d, cs_r, dt_r, spread = st["d"], st["cs_r"], st["dt_r"], st["spread"]
        mask = tri_lo if d == 0 else tri_up
        gl = slice(g * 256, (g + 1) * 256)
        cg = st["cm"][:, g * D_STATE:(g + 1) * D_STATE]
        btg = st["bt"][g * D_STATE:(g + 1) * D_STATE, :]
        cbg = _dot(cg, btg)
        htg = ht_s[d][g][...]
        y_off = _dot(cg, htg.astype(BF16)) * spread[:, SSD_HEADS * CHUNK + g * 256:SSD_HEADS * CHUNK + (g + 1) * 256]
        lms, bts, rstack = [], [], []
        cdec_x = jnp.zeros((1, 256), F32)
        for r in range(SSD_HPG):
            hh = g * SSD_HPG + r
            seg = spread[:, hh * CHUNK:(hh + 1) * CHUNK] - cs_r[hh:hh + 1, :]
            dec = jnp.exp(jnp.where(mask, seg, -jnp.inf))
            lms.append((cbg * dec * dt_r[hh:hh + 1, :]).astype(BF16))
            bts.append(btg * st["wst_r"][hh:hh + 1, :])
            rstack.append(st["xs_b"][:, gl] * head_masks[r])
            cdec_x = cdec_x + head_masks_f[r] * st["cdec"][hh:hh + 1, :]
        lhs = jnp.concatenate([jnp.concatenate(lms, axis=1), jnp.concatenate(bts, axis=1)], axis=0)
        res = _dot(lhs, jnp.concatenate(rstack, axis=0))
        st["ys"].append(res[0:CHUNK] + y_off)
        ht_s[d][g][...] = htg * cdec_x + res[CHUNK:2 * CHUNK]

    def scan_pair(cf, cb):
        sts = [scan_sums(cf, 0), scan_sums(cb, 1)]
        for st in sts:
            scan_spread(st)
        for g in range(SSD_GROUPS):
            for st in sts:
                scan_group(st, g)
        for st in sts:
            st["y"] = jnp.concatenate(st["ys"], axis=1)
        return sts

    for d in range(2):
        for g in range(SSD_GROUPS):
            ht_s[d][g][...] = h0_ref[0, d, g]
    cw = cw_ref[0]
    rows1 = lax.broadcasted_iota(I32, (CHUNK, 1), 0)

    def conv_body(c, carry):
        r0 = pl.multiple_of(c * CHUNK, CHUNK)
        x = xbc_ref[0, pl.ds(r0, CHUNK), :].astype(F32)
        rp0 = pl.multiple_of(jnp.maximum(r0 - 16, 0), 16)
        rn0 = pl.multiple_of(jnp.minimum(r0 + CHUNK, n - 16), 16)
        prev = xbc_ref[0, pl.ds(rp0, 16), :].astype(F32)[15:16, :]
        nxt = xbc_ref[0, pl.ds(rn0, 16), :].astype(F32)[0:1, :]
        prev = jnp.where(c > 0, prev, 0.0)
        nxt = jnp.where(c < nc - 1, nxt, 0.0)
        x_prev = jnp.where(rows1 == 0, prev, pltpu.roll(x, 1, 0))
        x_next = jnp.where(rows1 == CHUNK - 1, nxt, pltpu.roll(x, CHUNK - 1, 0))
        xc = _silu(cb_ref[0] + x_prev * cw[0:1] + x * cw[1:2] + x_next * cw[2:3])
        bm = xc[:, SSD_INNER:SSD_INNER + 256]
        bt = jnp.concatenate([bm[:, 0:D_STATE].T, bm[:, D_STATE:2 * D_STATE].T], axis=0).astype(BF16)
        xs_s[pl.ds(r0, CHUNK), :] = xc[:, 0:SSD_INNER]
        c_s[pl.ds(r0, CHUNK), :] = xc[:, SSD_INNER + 256:SSD_INNER + 512].astype(BF16)
        bt_s[:, pl.ds(r0, CHUNK)] = bt
        return carry

    lax.fori_loop(0, nc, conv_body, 0)

    def finish_chunk(st):
        r0, d = st["r0"], st["d"]
        y = yacc_s[1 - d][pl.ds(r0, CHUNK), :] + st["y"] + dsk_ref[0] * st["xs"]
        zz = z_ref[0, pl.ds(r0, CHUNK), :].astype(F32)
        g = y * _silu(zz)
        outs = [_rms_rows(g[:, gi * 256:(gi + 1) * 256]) for gi in range(SSD_GROUPS)]
        y_ref[0, pl.ds(r0, CHUNK), :] = (jnp.concatenate(outs, axis=1) * nw_ref[0]).astype(BF16)

    half = nc // 2

    def outer_body(i, carry):
        for st in scan_pair(i, nc - 1 - i):
            yacc_s[st["d"]][pl.ds(st["r0"], CHUNK), :] = st["y"]
        return carry

    def inner_body(i, carry):
        for st in scan_pair(half + i, half - 1 - i):
            finish_chunk(st)
        return carry

    lax.fori_loop(0, half, outer_body, 0)
    lax.fori_loop(0, half, inner_body, 0)
    for d in range(2):
        for g in range(SSD_GROUPS):
            hfin_ref[0, d, g] = ht_s[d][g][...]


def _ssd_mixer(xbc, z, dtt, h0, p, layer):
    b, n, _ = xbc.shape
    lsel3 = lambda bi: (layer, 0, 0)
    st_spec = pl.BlockSpec((1, 2, SSD_GROUPS, D_STATE, 256), lambda bi: (bi, 0, 0, 0, 0))
    return pl.pallas_call(
        functools.partial(_ssd_kernel, n=n),
        grid=(b,),
        in_specs=[
            pl.BlockSpec((1, n, CONV_DIM), lambda bi: (bi, 0, 0)),
            pl.BlockSpec((1, n, SSD_INNER), lambda bi: (bi, 0, 0)),
            pl.BlockSpec((1, N_DT, n), lambda bi: (bi, 0, 0)),
            pl.BlockSpec((1, 3, CONV_DIM), lsel3),
            pl.BlockSpec((1, 1, CONV_DIM), lsel3),
            pl.BlockSpec((1, N_DT, 1), lsel3),
            pl.BlockSpec((1, N_DT, 1), lsel3),
            pl.BlockSpec((1, 1, SSD_INNER), lsel3),
            pl.BlockSpec((1, 1, SSD_INNER), lsel3),
            pl.BlockSpec((CHUNK, SSD_HEADS * CHUNK + SSD_INNER), lambda bi: (0, 0)),
            st_spec,
        ],
        out_specs=[pl.BlockSpec((1, n, SSD_INNER), lambda bi: (bi, 0, 0)), st_spec],
        out_shape=[jax.ShapeDtypeStruct((b, n, SSD_INNER), BF16),
                   jax.ShapeDtypeStruct((b, 2, SSD_GROUPS, D_STATE, 256), F32)],
        scratch_shapes=[
            pltpu.VMEM((n, SSD_INNER), F32),
            pltpu.VMEM((n, 2 * D_STATE), BF16),
            pltpu.VMEM((2 * D_STATE, n), BF16),
            pltpu.VMEM((n, SSD_INNER), F32),
            pltpu.VMEM((n, SSD_INNER), F32),
        ] + [pltpu.VMEM((D_STATE, 256), F32)] * (2 * SSD_GROUPS),
        compiler_params=_cparams(("parallel",)),
        name="ssd_mixer",
    )(xbc, z, dtt, p["conv_w"], p["conv_b"], p["dt_bias_t"], p["a_log_t"], p["d_skip_x"], p["ssd_norm_w"],
      p["spread"], h0)


def _outproj_kernel(a_ref, s_ref, x_ref, mod_ref, wa_ref, ws_ref, nw_ref, wr_ref, xo_ref, h2_ref, aff_ref):
    mix = _dot(a_ref[0], wa_ref[0]) + _dot(s_ref[0], ws_ref[0])
    x = x_ref[0] + mod_ref[0, 0, 2:3, :] * mix
    xo_ref[0] = x
    h2 = (_rms_rows(x) * nw_ref[0] * (1.0 + mod_ref[0, 0, 4:5, :]) + mod_ref[0, 0, 3:4, :]).astype(BF16)
    h2_ref[0] = h2
    lg = _dot(h2, wr_ref[0])
    lgt = lg.T[0:N_EXPERTS, :]
    e = jnp.exp(lgt - jnp.max(lgt, axis=0, keepdims=True))
    aff_ref[0] = e / jnp.sum(e, axis=0, keepdims=True)


def _out_projection(attn, ssd, x, mod, mod_rows, layer, p, tm, group):
    b, n, d = x.shape
    nt = n // tm
    if mod_rows == 1:
        mod_map = lambda bi, i: (layer, mod.shape[1] - 1, 0, 0)
    else:
        mod_map = lambda bi, i: (layer, bi, 0, 0)
    lsel = lambda bi, i: (layer, 0, 0)
    row = lambda w: pl.BlockSpec((1, tm, w), lambda bi, i: (bi, i, 0))
    return pl.pallas_call(
        _outproj_kernel,
        grid=(b, nt),
        in_specs=[row(ATTN_WIDTH), row(SSD_INNER), row(d),
                  pl.BlockSpec((1, 1, N_MOD, d), mod_map),
                  pl.BlockSpec((1, ATTN_WIDTH, d), lsel),
                  pl.BlockSpec((1, SSD_INNER, d), lsel),
                  pl.BlockSpec((1, 1, d), lsel),
                  pl.BlockSpec((1, d, LANES), lsel)],
        out_specs=[row(d), row(d),
                   pl.BlockSpec((1, N_EXPERTS, tm), lambda bi, i: (bi // group, 0, (bi % group) * nt + i))],
        out_shape=[jax.ShapeDtypeStruct((b, n, d), F32),
                   jax.ShapeDtypeStruct((b, n, d), BF16),
                   jax.ShapeDtypeStruct((b // group, N_EXPERTS, group * n), F32)],
        compiler_params=_cparams(("parallel", "arbitrary")),
        name="out_projection",
    )(attn, ssd, x, mod, p["w_out_a"], p["w_out_s"], p["norm2_w"], p["w_router"])


def _route_kernel(aff_ref, slot_ref, *, n, group, cap):
    bits_all = pltpu.bitcast(aff_ref[0], I32)
    li = lax.broadcasted_iota(I32, (LANES, LANES), 0)
    lj = lax.broadcasted_iota(I32, (LANES, LANES), 1)
    strict_up = jnp.where(li < lj, 1.0, 0.0).astype(BF16)
    fcap = float(cap)

    def count(m):
        return jnp.sum(jnp.where(m, 1.0, 0.0), axis=-1, keepdims=True)

    def excl_prefix(m):
        total = jnp.zeros((N_EXPERTS, 1), F32)
        outs = []
        for j in range(n // LANES):
            blk = jnp.where(m[:, j * LANES:(j + 1) * LANES], 1.0, 0.0)
            outs.append(_dot(blk.astype(BF16), strict_up) + total)
            total = total + jnp.sum(blk, axis=-1, keepdims=True)
        return jnp.concatenate(outs, axis=1)

    for s in range(group):
        bits = bits_all[:, s * n:(s + 1) * n]

        def keep(cand, lo):
            return jnp.where(count(bits >= cand) >= fcap, cand, lo)

        def two_bits(i, lo):
            b1 = jnp.int32(1) << (30 - 2 * i)
            b0 = jnp.int32(1) << (29 - 2 * i)
            return keep(lo | b1 | b0, keep(lo | b1, keep(lo | b0, lo)))

        thr = lax.fori_loop(0, 15, two_bits, jnp.zeros((N_EXPERTS, 1), I32))
        thr = keep(thr | 1, thr)
        gt = bits > thr
        eq = bits == thr
        need = fcap - count(gt)
        sel = gt | (eq & (excl_prefix(eq) < need))
        pos = excl_prefix(sel)
        slot_ref[0, :, s * n:(s + 1) * n] = jnp.where(sel, pos.astype(I32) + s * cap, -1)


def _route(aff, n, group, cap):
    ng, _, t = aff.shape
    return pl.pallas_call(
        functools.partial(_route_kernel, n=n, group=group, cap=cap),
        grid=(ng,),
        in_specs=[pl.BlockSpec((1, N_EXPERTS, t), lambda g: (g, 0, 0))],
        out_specs=pl.BlockSpec((1, N_EXPERTS, t), lambda g: (g, 0, 0)),
        out_shape=jax.ShapeDtypeStruct((ng, N_EXPERTS, t), I32),
        compiler_params=_cparams(("parallel",)),
        name="ec_route",
    )(aff)


def _gather_kernel(slot_ref, aff_ref, h_ref, xe_ref, gate_ref, *, slots):
    t = h_ref.shape[1]
    ne = slot_ref.shape[2]
    sio = lax.broadcasted_iota(I32, (slots, t), 0)
    pm, gates = [], []
    for j in range(ne):
        hit = sio == slot_ref[0, 0, j:j + 1, :]
        pm.append(jnp.where(hit, 1.0, 0.0).astype(BF16))
        gates.append(jnp.sum(jnp.where(hit, aff_ref[0, 0, j:j + 1, :], 0.0), axis=-1, keepdims=True))
    xe = _dot(jnp.concatenate(pm, axis=0), h_ref[0])
    for j in range(ne):
        xe_ref[0, j] = xe[j * slots:(j + 1) * slots].astype(BF16)
        gate_ref[0, j] = jnp.broadcast_to(gates[j], (slots, LANES))


def _ffn_kernel(xe_ref, gate_ref, wg_ref, wu_ref, wd_ref, ye_ref):
    nb, _, s, d = xe_ref.shape
    xe = xe_ref[...].reshape(nb * s, d)
    gate = gate_ref[...].reshape(nb * s, LANES)[:, 0:1]
    hid = (_silu(_dot(xe, wg_ref[0, 0])) * _dot(xe, wu_ref[0, 0])).astype(BF16)
    ye = (_dot(hid, wd_ref[0, 0]) * gate).astype(BF16)
    ye_ref[...] = ye.reshape(nb, 1, s, d)


def _scatter_kernel(slot_ref, ye_ref, x_ref, mod_ref, o_ref, *, slots, chunk):
    tt = x_ref.shape[1]
    ne = slot_ref.shape[1]
    sio = lax.broadcasted_iota(I32, (slots, tt), 0)
    acc = jnp.zeros(x_ref.shape[1:], F32)
    for c0 in range(0, ne, chunk):
        pm = [jnp.where(sio == slot_ref[0, j:j + 1, :], 1.0, 0.0).astype(BF16) for j in range(c0, c0 + chunk)]
        ye = ye_ref[0, c0:c0 + chunk].reshape(chunk * slots, ye_ref.shape[3])
        acc = acc + _dot_tn(jnp.concatenate(pm, axis=0), ye)
    o_ref[0] = x_ref[0] + mod_ref[0, 0, 5:6, :] * acc


def _moe_residual(slot, aff, h2, x_mid, mod, mod_rows, p, layer, group):
    b, n, d = h2.shape
    ng, t = b // group, group * n
    cap = EC_CAPACITY * n // N_EXPERTS
    slots = group * cap
    ech = 4
    xe, gate = pl.pallas_call(
        functools.partial(_gather_kernel, slots=slots),
        grid=(ng, N_EXPERTS // ech),
        in_specs=[
            pl.BlockSpec((1, 1, ech, t), lambda g, c: (g, c, 0, 0)),
            pl.BlockSpec((1, 1, ech, t), lambda g, c: (g, c, 0, 0)),
            pl.BlockSpec((1, t, d), lambda g, c: (g, 0, 0)),
        ],
        out_specs=[pl.BlockSpec((1, ech, slots, d), lambda g, c: (g, c, 0, 0)),
                   pl.BlockSpec((1, ech, slots, LANES), lambda g, c: (g, c, 0, 0))],
        out_shape=[jax.ShapeDtypeStruct((ng, N_EXPERTS, slots, d), BF16),
                   jax.ShapeDtypeStruct((ng, N_EXPERTS, slots, LANES), F32)],
        compiler_params=_cparams(("parallel", "arbitrary")),
        name="ec_gather",
    )(slot.reshape(ng, N_EXPERTS // ech, ech, t), aff.reshape(ng, N_EXPERTS // ech, ech, t),
      h2.reshape(ng, t, d))

    nb = min(ng, 2)
    wsel = lambda e, g: (layer, e, 0, 0)
    ye = pl.pallas_call(
        _ffn_kernel,
        grid=(N_EXPERTS, ng // nb),
        in_specs=[
            pl.BlockSpec((nb, 1, slots, d), lambda e, g: (g, e, 0, 0)),
            pl.BlockSpec((nb, 1, slots, LANES), lambda e, g: (g, e, 0, 0)),
            pl.BlockSpec((1, 1, d, D_EXPERT), wsel),
            pl.BlockSpec((1, 1, d, D_EXPERT), wsel),
            pl.BlockSpec((1, 1, D_EXPERT, d), wsel),
        ],
        out_specs=pl.BlockSpec((nb, 1, slots, d), lambda e, g: (g, e, 0, 0)),
        out_shape=jax.ShapeDtypeStruct((ng, N_EXPERTS, slots, d), BF16),
        compiler_params=_cparams(("parallel", "arbitrary")),
        name="ec_ffn",
    )(xe, gate, p["w_gate"], p["w_up"], p["w_down"])

    tt = min(t, 512)
    nt = t // tt
    if mod_rows == 1:
        mod_map = lambda g, i: (layer, mod.shape[1] - 1, 0, 0)
    else:
        mod_map = lambda g, i: (layer, g, 0, 0)
    out = pl.pallas_call(
        functools.partial(_scatter_kernel, slots=slots, chunk=4),
        grid=(ng, nt),
        in_specs=[
            pl.BlockSpec((1, N_EXPERTS, tt), lambda g, i: (g, 0, i)),
            pl.BlockSpec((1, N_EXPERTS, slots, d), lambda g, i: (g, 0, 0, 0)),
            pl.BlockSpec((1, tt, d), lambda g, i: (g, i, 0)),
            pl.BlockSpec((1, 1, N_MOD, d), mod_map),
        ],
        out_specs=pl.BlockSpec((1, tt, d), lambda g, i: (g, i, 0)),
        out_shape=jax.ShapeDtypeStruct((ng, t, d), F32),
        compiler_params=_cparams(("parallel", "arbitrary")),
        name="ec_scatter",
    )(slot, ye, x_mid.reshape(ng, t, d), mod)
    return out.reshape(b, n, d)


def _rope_tables(n):
    rows = n // GRID_W
    pos = np.stack([np.repeat(np.arange(rows), GRID_W), np.tile(np.arange(GRID_W), rows)], axis=-1)
    n_freq = HEAD_DIM // 4
    inv = jnp.asarray(ROPE_BASE, F32) ** (-jnp.arange(n_freq, dtype=F32) / n_freq)
    ang = jnp.asarray(pos, F32)[..., None] * inv
    cos, sin = jnp.cos(ang), jnp.sin(ang)
    zero = jnp.zeros_like(sin)
    cos64 = jnp.concatenate([cos, cos], axis=-1).reshape(n, HEAD_DIM)
    sa64 = jnp.concatenate([-sin, zero], axis=-1).reshape(n, HEAD_DIM)
    sb64 = jnp.concatenate([zero, sin], axis=-1).reshape(n, HEAD_DIM)
    two = lambda t: jnp.concatenate([t, t], axis=-1)
    return two(cos64), two(sa64), two(sb64)


def _prepare(w_in, norm1_w, norm2_w, q_norm_w, k_norm_w, lambda_q1, lambda_k1, lambda_q2, lambda_k2, subln_w,
             conv_w, conv_b, dt_bias_f, dt_bias_b, a_log_f, a_log_b, d_skip, ssd_norm_w, w_out, w_router,
             w_gate, w_up, w_down):
    nl = w_in.shape[0]
    gi = np.arange(MXU_WIDTH) // HEAD_DIM
    dt_bias = jnp.concatenate([dt_bias_f, dt_bias_b], axis=-1)
    a_log = jnp.concatenate([a_log_f, a_log_b], axis=-1)
    tile2 = lambda w: jnp.concatenate([w, w], axis=-1).reshape(nl, 1, LANES)
    qk_max = jnp.max(jnp.abs(q_norm_w), axis=-1) * jnp.max(jnp.abs(k_norm_w), axis=-1)
    kk = np.arange(CHUNK)[:, None]
    jj = np.arange(SSD_HEADS * CHUNK + SSD_INNER)[None, :]
    to_cs = (kk < 3 * SSD_HEADS) & (jj < SSD_HEADS * CHUNK) & (kk % SSD_HEADS == jj // CHUNK)
    to_ecs = ((kk >= 3 * SSD_HEADS) & (kk < 6 * SSD_HEADS) & (jj >= SSD_HEADS * CHUNK)
              & (kk % SSD_HEADS == (jj - SSD_HEADS * CHUNK) // SSD_HEADDIM))
    return {
        "w_main": w_in[:, :, :P_MAIN].astype(BF16),
        "w_dtt": jnp.swapaxes(w_in[:, :, P_MAIN:], 1, 2).astype(BF16),
        "gmat": jnp.asarray(gi[:, None] == gi[None, :], BF16),
        "norm1_w": norm1_w.reshape(nl, 1, D_MODEL),
        "norm2_w": norm2_w.reshape(nl, 1, D_MODEL),
        "q_norm_w": tile2(q_norm_w),
        "k_norm_w": tile2(k_norm_w),
        "lam": jnp.stack([lambda_q1, lambda_k1, lambda_q2, lambda_k2], axis=1),
        "subln_w": subln_w.reshape(nl, 1, ATTN_VDIM),
        "logit_bound": LOGIT_MARGIN * HEAD_DIM ** 0.5 * qk_max,
        "conv_w": conv_w,
        "conv_b": conv_b.reshape(nl, 1, CONV_DIM),
        "dt_bias_t": dt_bias.reshape(nl, N_DT, 1),
        "a_log_t": a_log.reshape(nl, N_DT, 1),
        "d_skip_x": jnp.repeat(d_skip, SSD_HEADDIM, axis=-1).reshape(nl, 1, SSD_INNER),
        "ssd_norm_w": ssd_norm_w.reshape(nl, 1, SSD_INNER),
        "spread": jnp.asarray(to_cs | to_ecs, BF16),
        "w_out_a": w_out[:, :ATTN_WIDTH, :].astype(BF16),
        "w_out_s": w_out[:, ATTN_WIDTH:, :].astype(BF16),
        "w_router": jnp.pad(w_router, ((0, 0), (0, 0), (0, LANES - N_EXPERTS))).astype(BF16),
        "w_gate": w_gate.astype(BF16),
        "w_up": w_up.astype(BF16),
        "w_down": w_down.astype(BF16),
    }


def _ffn(h2, aff, x_mid, mod, mod_rows, layer, p, group):
    n = h2.shape[1]
    slot = _route(aff, n, group, EC_CAPACITY * n // N_EXPERTS)
    return _moe_residual(slot, aff, h2, x_mid, mod, mod_rows, p, layer, group)


def kernel(x, c, ctx, c_ctx, w_mod, b_mod, norm1_w, norm2_w, w_in, q_norm_w, k_norm_w, lambda_q1, lambda_k1,
           lambda_q2, lambda_k2, subln_w, conv_w, conv_b, dt_bias_f, dt_bias_b, a_log_f, a_log_b, d_skip,
           ssd_norm_w, w_out, w_router, w_gate, w_up, w_down):
    b, n, d = x.shape
    m = ctx.shape[1]
    nl = w_in.shape[0]
    assert d == D_MODEL and n % GRID_W == 0 and n % (2 * CHUNK) == 0 and m % (2 * CHUNK) == 0 and n % m == 0
    assert MOE_TOKENS % n == 0 and MOE_TOKENS % m == 0
    g_lat, g_ctx = MOE_TOKENS // n, MOE_TOKENS // m
    assert b % g_lat == 0 and b % g_ctx == 0
    tm_lat, tm_ctx = min(n, 512), min(m, 256)
    tq = min(n, 512)

    p = _prepare(w_in, norm1_w, norm2_w, q_norm_w, k_norm_w, lambda_q1, lambda_k1, lambda_q2, lambda_k2,
                 subln_w, conv_w, conv_b, dt_bias_f, dt_bias_b, a_log_f, a_log_b, d_skip, ssd_norm_w, w_out,
                 w_router, w_gate, w_up, w_down)
    rope = _rope_tables(n)
    mod = _modulation(jnp.concatenate([c, c_ctx[None, :]], axis=0), w_mod, b_mod)
    h_zero = jnp.zeros((b, 2, SSD_GROUPS, D_STATE, 256), F32)

    x_lat, x_ctx = x, ctx
    for l in range(nl):
        update_ctx = l < nl - 1
        lam_init = 0.8 - 0.6 * math.exp(-0.3 * l)
        q, k_all, v_all, z, xbc, dtt = _in_projection(x_lat, mod, b, l, p, tm_lat, n + m, rope_tabs=rope)
        qc, k_all, v_all, zc, xbcc, dttc = _in_projection(x_ctx, mod, 1, l, p, tm_ctx, n + m, kv=(k_all, v_all))
        ssd_c, h_ctx = _ssd_mixer(xbcc, zc, dttc, h_zero, p, l)
        ssd, _ = _ssd_mixer(xbc, z, dtt, h_ctx, p, l)
        attn = _diff_attention(q, k_all, v_all, p, l, lam_init, tq, n + m, 0)
        x_mid, h2, aff = _out_projection(attn, ssd, x_lat, mod, b, l, p, tm_lat, g_lat)
        x_lat = _ffn(h2, aff, x_mid, mod, b, l, p, g_lat)
        if update_ctx:
            attn_c = _diff_attention(qc, k_all, v_all, p, l, lam_init, min(m, 256), m, n // m)
            xc_mid, hc2, aff_c = _out_projection(attn_c, ssd_c, x_ctx, mod, 1, l, p, tm_ctx, g_ctx)
            x_ctx = _ffn(hc2, aff_c, xc_mid, mod, 1, l, p, g_ctx)
    return x_lat
```

```python
import functools
import math

import jax
import jax.numpy as jnp
import numpy as np
from jax import lax
from jax.experimental import pallas as pl
from jax.experimental.pallas import tpu as pltpu

F32 = jnp.float32
BF16 = jnp.bfloat16
I32 = jnp.int32

D_MODEL = 1024
DEPTH = 4
GRID_W = 64
N_MOD = 6
EPS = 1e-6
ATTN_HEADS = 4
HEAD_DIM = 64
ATTN_VDIM = 128
Q_WIDTH = 512
ATTN_WIDTH = 512
ROPE_BASE = 10000.0
SSD_HEADDIM = 64
SSD_GROUPS = 2
SSD_HPG = 4
SSD_HEADS = 8
SSD_INNER = 512
D_STATE = 128
CHUNK = 128
CONV_DIM = 1024
P_MAIN = 3072
N_DT = 2 * SSD_HEADS
N_EXPERTS = 16
EC_CAPACITY = 2
D_EXPERT = 1408
LANES = 128
MXU_WIDTH = 256
MOE_TOKENS = 2048
QK_SCALE = HEAD_DIM ** -0.5 * math.log2(math.e)
LOGIT_MARGIN = 1.03
ATTN_SAFE_LOGIT = 30.0
VMEM_LIMIT = 56 * 1024 * 1024


def _cparams(sem):
    return pltpu.CompilerParams(dimension_semantics=sem, vmem_limit_bytes=VMEM_LIMIT)


def _sigmoid(x):
    return 0.5 * jnp.tanh(0.5 * x) + 0.5


def _silu(x):
    return x * _sigmoid(x)


def _softplus(x):
    return jnp.maximum(x, 0.0) + jnp.log1p(jnp.exp(-jnp.abs(x)))


def _dot(a, b):
    return jnp.dot(a, b, preferred_element_type=F32)


def _dot_nt(a, b):
    return lax.dot_general(a, b, (((1,), (1,)), ((), ())), preferred_element_type=F32)


def _dot_tn(a, b):
    return lax.dot_general(a, b, (((0,), (0,)), ((), ())), preferred_element_type=F32)


def _split3(x):
    hi = x.astype(BF16)
    r = x - hi.astype(F32)
    mid = r.astype(BF16)
    lo = (r - mid.astype(F32)).astype(BF16)
    return hi, mid, lo


def _mod_kernel(cc_ref, w_ref, b_ref, o_ref):
    s = _silu(cc_ref[...]).astype(BF16)
    o_ref[0] = _dot(s, w_ref[0].astype(BF16)) + b_ref[0]


def _modulation(cc, w_mod, b_mod):
    nl, d, nm = w_mod.shape
    r = cc.shape[0]
    tn = 1536
    out = pl.pallas_call(
        _mod_kernel,
        grid=(nl, nm // tn),
        in_specs=[
            pl.BlockSpec((r, d), lambda l, j: (0, 0)),
            pl.BlockSpec((1, d, tn), lambda l, j: (l, 0, j)),
            pl.BlockSpec((1, 1, tn), lambda l, j: (l, 0, j)),
        ],
        out_specs=pl.BlockSpec((1, r, tn), lambda l, j: (l, 0, j)),
        out_shape=jax.ShapeDtypeStruct((nl, r, nm), F32),
        compiler_params=_cparams(("arbitrary", "arbitrary")),
        name="modulation",
    )(cc, w_mod, b_mod.reshape(nl, 1, nm))
    return out.reshape(nl, r, N_MOD, d)


def _rms_rows(x):
    return x * lax.rsqrt(jnp.mean(x * x, axis=-1, keepdims=True) + EPS)


def _head_norm(t, gmat, w128, rope_tabs, scale):
    t2 = (t * t).astype(BF16)
    gw = gmat.shape[0]
    ss = jnp.concatenate([_dot(t2[:, j:j + gw], gmat) for j in range(0, Q_WIDTH, gw)], axis=1)
    y = t * lax.rsqrt(ss * (1.0 / HEAD_DIM) + EPS)
    outs = []
    for j in range(Q_WIDTH // LANES):
        yj = y[:, j * LANES:(j + 1) * LANES] * w128
        if rope_tabs is not None:
            cos, sa, sb = rope_tabs
            up = pltpu.roll(yj, LANES - 16, 1)
            dn = pltpu.roll(yj, 16, 1)
            yj = yj * cos + up * sa + dn * sb
        outs.append(yj * scale)
    return jnp.concatenate(outs, axis=1).astype(BF16)


def _inproj_kernel(*refs, rope):
    if rope:
        (x_ref, mod_ref, nw_ref, w_ref, wdtt_ref, g_ref, qw_ref, kw_ref,
         cos_ref, sa_ref, sb_ref, q_o, k_o, v_o, z_o, xbc_o, dtt_o) = refs
        tabs = (cos_ref[...], sa_ref[...], sb_ref[...])
    else:
        (x_ref, mod_ref, nw_ref, w_ref, wdtt_ref, g_ref, qw_ref, kw_ref, _, _,
         q_o, k_o, v_o, z_o, xbc_o, dtt_o) = refs
        tabs = None
    x = x_ref[0]
    sh = mod_ref[0, 0, 0:1, :]
    sc = mod_ref[0, 0, 1:2, :]
    h = (_rms_rows(x) * nw_ref[0] * (1.0 + sc) + sh).astype(BF16)
    gmat = g_ref[...]
    pq = _dot(h, w_ref[0, :, 0:512])
    pk = _dot(h, w_ref[0, :, 512:1024])
    q_o[0] = _head_norm(pq, gmat, qw_ref[0], tabs, QK_SCALE)
    pv = _dot(h, w_ref[0, :, 1024:1536])
    k_o[0] = _head_norm(pk, gmat, kw_ref[0], tabs, 1.0)
    pz = _dot(h, w_ref[0, :, 1536:2048])
    v_o[0] = pv.astype(BF16)
    px = _dot(h, w_ref[0, :, 2048:3072])
    z_o[0] = pz.astype(BF16)
    dtt_o[0] = _dot_nt(wdtt_ref[0], h)
    xbc_o[0] = px.astype(BF16)


def _in_projection(x, mod, mod_rows, layer, p, tm, kv_rows, rope_tabs=None, kv=None):
    b, n, d = x.shape
    rope = rope_tabs is not None
    assert rope != (kv is not None)
    kv_blk0 = 0 if rope else (kv_rows - n) // tm
    if mod_rows == 1:
        mod_map = lambda bi, i: (layer, mod.shape[1] - 1, 0, 0)
    else:
        mod_map = lambda bi, i: (layer, bi, 0, 0)
    lsel = lambda bi, i: (layer, 0, 0)
    in_specs = [
        pl.BlockSpec((1, tm, d), lambda bi, i: (bi, i, 0)),
        pl.BlockSpec((1, 1, N_MOD, d), mod_map),
        pl.BlockSpec((1, 1, d), lsel),
        pl.BlockSpec((1, d, P_MAIN), lsel),
        pl.BlockSpec((1, N_DT, d), lsel),
        pl.BlockSpec((MXU_WIDTH, MXU_WIDTH), lambda bi, i: (0, 0)),
        pl.BlockSpec((1, 1, LANES), lsel),
        pl.BlockSpec((1, 1, LANES), lsel),
    ]
    args = [x, mod, p["norm1_w"], p["w_main"], p["w_dtt"], p["gmat"], p["q_norm_w"], p["k_norm_w"]]
    if rope:
        in_specs += [pl.BlockSpec((tm, LANES), lambda bi, i: (i, 0))] * 3
        args += list(rope_tabs)
        aliases = {}
    else:
        in_specs += [pl.BlockSpec(memory_space=pl.ANY)] * 2
        aliases = {len(args): 1, len(args) + 1: 2}
        args += list(kv)
    row = lambda w: pl.BlockSpec((1, tm, w), lambda bi, i: (bi, i, 0))
    kv_row = pl.BlockSpec((1, tm, 512), lambda bi, i: (bi, kv_blk0 + i, 0))
    out_specs = [row(512), kv_row, kv_row, row(512), row(1024),
                 pl.BlockSpec((1, N_DT, tm), lambda bi, i: (bi, 0, i))]
    out_shape = [jax.ShapeDtypeStruct((b, n, 512), BF16),
                 jax.ShapeDtypeStruct((b, kv_rows, 512), BF16),
                 jax.ShapeDtypeStruct((b, kv_rows, 512), BF16),
                 jax.ShapeDtypeStruct((b, n, 512), BF16),
                 jax.ShapeDtypeStruct((b, n, 1024), BF16),
                 jax.ShapeDtypeStruct((b, N_DT, n), F32)]
    return pl.pallas_call(
        functools.partial(_inproj_kernel, rope=rope),
        grid=(b, n // tm),
        in_specs=in_specs,
        out_specs=out_specs,
        out_shape=out_shape,
        input_output_aliases=aliases,
        compiler_params=_cparams(("parallel", "arbitrary")),
        name="in_projection_rope" if rope else "in_projection",
    )(*args)


def _attn_kernel(bound_ref, q_ref, k_ref, v_ref, lam_ref, sw_ref, o_ref, *, layer, lam_init):
    tq = q_ref.shape[1]
    lp = lam_ref[0]
    lam = (jnp.exp(jnp.sum(lp[0:1] * lp[1:2], axis=-1, keepdims=True))
           - jnp.exp(jnp.sum(lp[2:3] * lp[3:4], axis=-1, keepdims=True)) + lam_init)
    lane = lax.broadcasted_iota(I32, (1, LANES), 1)
    m_lo = jnp.where(lane < HEAD_DIM, 1.0, 0.0).astype(BF16)
    m_hi = jnp.where(lane < HEAD_DIM, 0.0, 1.0).astype(BF16)
    sw = sw_ref[0]

    def heads(shift_by_max):
        for h in range(ATTN_HEADS):
            cols = slice(h * LANES, (h + 1) * LANES)
            qh = q_ref[0, :, cols]
            qs = jnp.concatenate([qh * m_lo, qh * m_hi], axis=0)
            s = _dot_nt(qs, k_ref[0, :, cols])
            if shift_by_max:
                s = s - jnp.max(s, axis=-1, keepdims=True)
            p = jnp.exp2(s)
            r = 1.0 / jnp.sum(p, axis=-1, keepdims=True)
            pv = _dot(p.astype(BF16), v_ref[0, :, cols]) * r
            o = pv[0:tq] - lam * pv[tq:2 * tq]
            o = _rms_rows(o) * sw * (1.0 - lam_init)
            o_ref[0, :, cols] = o.astype(BF16)

    small = bound_ref[layer] < ATTN_SAFE_LOGIT

    @pl.when(small)
    def _():
        heads(False)

    @pl.when(jnp.logical_not(small))
    def _():
        heads(True)


def _diff_attention(q, k_all, v_all, p, layer, lam_init, tq, nk, kv_block):
    b, n, _ = q.shape
    return pl.pallas_call(
        functools.partial(_attn_kernel, layer=layer, lam_init=lam_init),
        grid=(b, n // tq),
        in_specs=[
            pl.BlockSpec(memory_space=pltpu.SMEM),
            pl.BlockSpec((1, tq, Q_WIDTH), lambda bi, i: (bi, i, 0)),
            pl.BlockSpec((1, nk, Q_WIDTH), lambda bi, i: (bi, kv_block, 0)),
            pl.BlockSpec((1, nk, ATTN_WIDTH), lambda bi, i: (bi, kv_block, 0)),
            pl.BlockSpec((1, 4, HEAD_DIM), lambda bi, i: (layer, 0, 0)),
            pl.BlockSpec((1, 1, ATTN_VDIM), lambda bi, i: (layer, 0, 0)),
        ],
        out_specs=pl.BlockSpec((1, tq, ATTN_WIDTH), lambda bi, i: (bi, i, 0)),
        out_shape=jax.ShapeDtypeStruct((b, n, ATTN_WIDTH), BF16),
        compiler_params=_cparams(("parallel", "arbitrary")),
        name="diff_attention",
    )(p["logit_bound"], q, k_all, v_all, p["lam"], p["subln_w"])


def _ssd_kernel(xbc_ref, z_ref, dtt_ref, cw_ref, cb_ref, dtbt_ref, alt_ref, dsk_ref, nw_ref, sp_ref, h0_ref,
                y_ref, hfin_ref, xs_s, c_s, bt_s, yf_s, yb_s, hf0_s, hf1_s, hb0_s, hb1_s, *, n):
    nc = n // CHUNK
    ht_s = ((hf0_s, hf1_s), (hb0_s, hb1_s))
    yacc_s = (yf_s, yb_s)
    row = lax.broadcasted_iota(I32, (CHUNK, CHUNK), 0)
    col = lax.broadcasted_iota(I32, (CHUNK, CHUNK), 1)
    tri_lo = col <= row
    tri_up = col >= row
    t_lo = jnp.where(tri_lo, 1.0, 0.0).astype(BF16)
    t_up = jnp.where(tri_up, 1.0, 0.0).astype(BF16)
    a_col = -jnp.exp(alt_ref[0])
    lane_head = lax.broadcasted_iota(I32, (1, SSD_INNER // 2), 1) // SSD_HEADDIM
    head_masks_f = [jnp.where(lane_head == r, 1.0, 0.0) for r in range(SSD_HPG)]
    head_masks = [m.astype(BF16) for m in head_masks_f]
    zero_rows = jnp.zeros((CHUNK - 6 * SSD_HEADS, CHUNK), F32)
    spread_mat = sp_ref[...]


    def scan_sums(c, d):
        r0 = pl.multiple_of(c * CHUNK, CHUNK)
        hs = slice(d * SSD_HEADS, (d + 1) * SSD_HEADS)
        dt_r = _softplus(dtt_ref[0, hs, pl.ds(r0, CHUNK)] + dtbt_ref[0, hs, :])
        hi, mid, lo = _split3(dt_r * a_col[hs])
        da3 = jnp.concatenate([hi.astype(F32), mid.astype(F32), lo.astype(F32)], axis=0).astype(BF16)
        cs3 = _dot(da3, t_up if d == 0 else t_lo)
        cs_r = cs3[0:8] + cs3[8:16] + cs3[16:24]
        edge = CHUNK - 1 if d == 0 else 0
        cs_edge = cs_r[:, edge:edge + 1]
        return dict(c=c, d=d, r0=r0, dt_r=dt_r, cs_r=cs_r,
                    wst_r=(dt_r * jnp.exp(cs_edge - cs_r)).astype(BF16),
                    cdec=jnp.exp(cs_edge))

    def scan_spread(st):
        cs_r, r0 = st["cs_r"], st["r0"]
        pieces = [t.astype(F32) for t in _split3(cs_r) + _split3(jnp.exp(cs_r))]
        colf = jnp.concatenate(pieces + [zero_rows], axis=0).T.astype(BF16)
        st["spread"] = _dot(colf, spread_mat)
        st["xs"] = xs_s[pl.ds(r0, CHUNK), :]
        st["xs_b"] = st["xs"].astype(BF16)
        st["cm"] = c_s[pl.ds(r0, CHUNK), :]
        st["bt"] = bt_s[:, pl.ds(r0, CHUNK)]
        st["ys"] = []

    def scan_group(st, g):
        d, cs_r, dt_r, spread = st["d"], st["cs_r"], st["dt_r"], st["spread"]
        mask = tri_lo if d == 0 else tri_up
        gl = slice(g * 256, (g + 1) * 256)
        cg = st["cm"][:, g * D_STATE:(g + 1) * D_STATE]
        btg = st["bt"][g * D_STATE:(g + 1) * D_STATE, :]
        cbg = _dot(cg, btg)
        htg = ht_s[d][g][...]
        y_off = _dot(cg, htg.astype(BF16)) * spread[:, SSD_HEADS * CHUNK + g * 256:SSD_HEADS * CHUNK + (g + 1) * 256]
        lms, bts, rstack = [], [], []
        cdec_x = jnp.zeros((1, 256), F32)
        for r in range(SSD_HPG):
            hh = g * SSD_HPG + r
            seg = spread[:, hh * CHUNK:(hh + 1) * CHUNK] - cs_r[hh:hh + 1, :]
            dec = jnp.exp(jnp.where(mask, seg, -jnp.inf))
            lms.append((cbg * dec * dt_r[hh:hh + 1, :]).astype(BF16))
            bts.append(btg * st["wst_r"][hh:hh + 1, :])
            rstack.append(st["xs_b"][:, gl] * head_masks[r])
            cdec_x = cdec_x + head_masks_f[r] * st["cdec"][hh:hh + 1, :]
        lhs = jnp.concatenate([jnp.concatenate(lms, axis=1), jnp.concatenate(bts, axis=1)], axis=0)
        res = _dot(lhs, jnp.concatenate(rstack, axis=0))
        st["ys"].append(res[0:CHUNK] + y_off)
        ht_s[d][g][...] = htg * cdec_x + res[CHUNK:2 * CHUNK]

    def scan_pair(cf, cb):
        sts = [scan_sums(cf, 0), scan_sums(cb, 1)]
        for st in sts:
            scan_spread(st)
        for g in range(SSD_GROUPS):
            for st in sts:
                scan_group(st, g)
        for st in sts:
            st["y"] = jnp.concatenate(st["ys"], axis=1)
        return sts

    for d in range(2):
        for g in range(SSD_GROUPS):
            ht_s[d][g][...] = h0_ref[0, d, g]
    cw = cw_ref[0]
    rows1 = lax.broadcasted_iota(I32, (CHUNK, 1), 0)

    def conv_body(c, carry):
        r0 = pl.multiple_of(c * CHUNK, CHUNK)
        x = xbc_ref[0, pl.ds(r0, CHUNK), :].astype(F32)
        rp0 = pl.multiple_of(jnp.maximum(r0 - 16, 0), 16)
        rn0 = pl.multiple_of(jnp.minimum(r0 + CHUNK, n - 16), 16)
        prev = xbc_ref[0, pl.ds(rp0, 16), :].astype(F32)[15:16, :]
        nxt = xbc_ref[0, pl.ds(rn0, 16), :].astype(F32)[0:1, :]
        prev = jnp.where(c > 0, prev, 0.0)
        nxt = jnp.where(c < nc - 1, nxt, 0.0)
        x_prev = jnp.where(rows1 == 0, prev, pltpu.roll(x, 1, 0))
        x_next = jnp.where(rows1 == CHUNK - 1, nxt, pltpu.roll(x, CHUNK - 1, 0))
        xc = _silu(cb_ref[0] + x_prev * cw[0:1] + x * cw[1:2] + x_next * cw[2:3])
        bm = xc[:, SSD_INNER:SSD_INNER + 256]
        bt = jnp.concatenate([bm[:, 0:D_STATE].T, bm[:, D_STATE:2 * D_STATE].T], axis=0).astype(BF16)
        xs_s[pl.ds(r0, CHUNK), :] = xc[:, 0:SSD_INNER]
        c_s[pl.ds(r0, CHUNK), :] = xc[:, SSD_INNER + 256:SSD_INNER + 512].astype(BF16)
        bt_s[:, pl.ds(r0, CHUNK)] = bt
        return carry

    lax.fori_loop(0, nc, conv_body, 0)

    def finish_chunk(st):
        r0, d = st["r0"], st["d"]
        y = yacc_s[1 - d][pl.ds(r0, CHUNK), :] + st["y"] + dsk_ref[0] * st["xs"]
        zz = z_ref[0, pl.ds(r0, CHUNK), :].astype(F32)
        g = y * _silu(zz)
        outs = [_rms_rows(g[:, gi * 256:(gi + 1) * 256]) for gi in range(SSD_GROUPS)]
        y_ref[0, pl.ds(r0, CHUNK), :] = (jnp.concatenate(outs, axis=1) * nw_ref[0]).astype(BF16)

    half = nc // 2

    def outer_body(i, carry):
        for st in scan_pair(i, nc - 1 - i):
            yacc_s[st["d"]][pl.ds(st["r0"], CHUNK), :] = st["y"]
        return carry

    def inner_body(i, carry):
        for st in scan_pair(half + i, half - 1 - i):
            finish_chunk(st)
        return carry

    lax.fori_loop(0, half, outer_body, 0)
    lax.fori_loop(0, half, inner_body, 0)
    for d in range(2):
        for g in range(SSD_GROUPS):
            hfin_ref[0, d, g] = ht_s[d][g][...]


def _ssd_mixer(xbc, z, dtt, h0, p, layer):
    b, n, _ = xbc.shape
    lsel3 = lambda bi: (layer, 0, 0)
    st_spec = pl.BlockSpec((1, 2, SSD_GROUPS, D_STATE, 256), lambda bi: (bi, 0, 0, 0, 0))
    return pl.pallas_call(
        functools.partial(_ssd_kernel, n=n),
        grid=(b,),
        in_specs=[
            pl.BlockSpec((1, n, CONV_DIM), lambda bi: (bi, 0, 0)),
            pl.BlockSpec((1, n, SSD_INNER), lambda bi: (bi, 0, 0)),
            pl.BlockSpec((1, N_DT, n), lambda bi: (bi, 0, 0)),
            pl.BlockSpec((1, 3, CONV_DIM), lsel3),
            pl.BlockSpec((1, 1, CONV_DIM), lsel3),
            pl.BlockSpec((1, N_DT, 1), lsel3),
            pl.BlockSpec((1, N_DT, 1), lsel3),
            pl.BlockSpec((1, 1, SSD_INNER), lsel3),
            pl.BlockSpec((1, 1, SSD_INNER), lsel3),
            pl.BlockSpec((CHUNK, SSD_HEADS * CHUNK + SSD_INNER), lambda bi: (0, 0)),
            st_spec,
        ],
        out_specs=[pl.BlockSpec((1, n, SSD_INNER), lambda bi: (bi, 0, 0)), st_spec],
        out_shape=[jax.ShapeDtypeStruct((b, n, SSD_INNER), BF16),
                   jax.ShapeDtypeStruct((b, 2, SSD_GROUPS, D_STATE, 256), F32)],
        scratch_shapes=[
            pltpu.VMEM((n, SSD_INNER), F32),
            pltpu.VMEM((n, 2 * D_STATE), BF16),
            pltpu.VMEM((2 * D_STATE, n), BF16),
            pltpu.VMEM((n, SSD_INNER), F32),
            pltpu.VMEM((n, SSD_INNER), F32),
        ] + [pltpu.VMEM((D_STATE, 256), F32)] * (2 * SSD_GROUPS),
        compiler_params=_cparams(("parallel",)),
        name="ssd_mixer",
    )(xbc, z, dtt, p["conv_w"], p["conv_b"], p["dt_bias_t"], p["a_log_t"], p["d_skip_x"], p["ssd_norm_w"],
      p["spread"], h0)


def _outproj_kernel(a_ref, s_ref, x_ref, mod_ref, wa_ref, ws_ref, nw_ref, wr_ref, xo_ref, h2_ref, aff_ref):
    tm = x_ref.shape[1]
    half = tm // 2 if tm >= 512 else tm
    blocks = [slice(r, r + half) for r in range(0, tm, half)]
    mixes = [_dot(a_ref[0, rows, :], wa_ref[0]) + _dot(s_ref[0, rows, :], ws_ref[0]) for rows in blocks]
    for rows, mix in zip(blocks, mixes):
        x = x_ref[0, rows, :] + mod_ref[0, 0, 2:3, :] * mix
        xo_ref[0, rows, :] = x
        h2 = (_rms_rows(x) * nw_ref[0] * (1.0 + mod_ref[0, 0, 4:5, :]) + mod_ref[0, 0, 3:4, :]).astype(BF16)
        h2_ref[0, rows, :] = h2
        lg = _dot(h2, wr_ref[0])
        lgt = lg.T[0:N_EXPERTS, :]
        e = jnp.exp(lgt - jnp.max(lgt, axis=0, keepdims=True))
        aff_ref[0, :, rows] = e / jnp.sum(e, axis=0, keepdims=True)


def _out_projection(attn, ssd, x, mod, mod_rows, layer, p, tm, group):
    b, n, d = x.shape
    nt = n // tm
    if mod_rows == 1:
        mod_map = lambda bi, i: (layer, mod.shape[1] - 1, 0, 0)
    else:
        mod_map = lambda bi, i: (layer, bi, 0, 0)
    lsel = lambda bi, i: (layer, 0, 0)
    row = lambda w: pl.BlockSpec((1, tm, w), lambda bi, i: (bi, i, 0))
    return pl.pallas_call(
        _outproj_kernel,
        grid=(b, nt),
        in_specs=[row(ATTN_WIDTH), row(SSD_INNER), row(d),
                  pl.BlockSpec((1, 1, N_MOD, d), mod_map),
                  pl.BlockSpec((1, ATTN_WIDTH, d), lsel),
                  pl.BlockSpec((1, SSD_INNER, d), lsel),
                  pl.BlockSpec((1, 1, d), lsel),
                  pl.BlockSpec((1, d, LANES), lsel)],
        out_specs=[row(d), row(d),
                   pl.BlockSpec((1, N_EXPERTS, tm), lambda bi, i: (bi // group, 0, (bi % group) * nt + i))],
        out_shape=[jax.ShapeDtypeStruct((b, n, d), F32),
                   jax.ShapeDtypeStruct((b, n, d), BF16),
                   jax.ShapeDtypeStruct((b // group, N_EXPERTS, group * n), F32)],
        compiler_params=_cparams(("parallel", "arbitrary")),
        name="out_projection",
    )(attn, ssd, x, mod, p["w_out_a"], p["w_out_s"], p["norm2_w"], p["w_router"])


def _route_kernel(aff_ref, slot_ref, *, n, group, cap):
    bits_all = pltpu.bitcast(aff_ref[0], I32)
    li = lax.broadcasted_iota(I32, (LANES, LANES), 0)
    lj = lax.broadcasted_iota(I32, (LANES, LANES), 1)
    strict_up = jnp.where(li < lj, 1.0, 0.0).astype(BF16)
    fcap = float(cap)

    def count(m):
        return jnp.sum(jnp.where(m, 1.0, 0.0), axis=-1, keepdims=True)

    def excl_prefix(m):
        total = jnp.zeros((N_EXPERTS, 1), F32)
        outs = []
        for j in range(n // LANES):
            blk = jnp.where(m[:, j * LANES:(j + 1) * LANES], 1.0, 0.0)
            outs.append(_dot(blk.astype(BF16), strict_up) + total)
            total = total + jnp.sum(blk, axis=-1, keepdims=True)
        return jnp.concatenate(outs, axis=1)

    for s in range(group):
        bits = bits_all[:, s * n:(s + 1) * n]

        def keep(cand, lo):
            return jnp.where(count(bits >= cand) >= fcap, cand, lo)

        def two_bits(i, lo):
            b1 = jnp.int32(1) << (30 - 2 * i)
            b0 = jnp.int32(1) << (29 - 2 * i)
            return keep(lo | b1 | b0, keep(lo | b1, keep(lo | b0, lo)))

        thr = lax.fori_loop(0, 15, two_bits, jnp.zeros((N_EXPERTS, 1), I32))
        thr = keep(thr | 1, thr)
        gt = bits > thr
        eq = bits == thr
        need = fcap - count(gt)
        sel = gt | (eq & (excl_prefix(eq) < need))
        pos = excl_prefix(sel)
        slot_ref[0, :, s * n:(s + 1) * n] = jnp.where(sel, pos.astype(I32) + s * cap, -1)


def _route(aff, n, group, cap):
    ng, _, t = aff.shape
    return pl.pallas_call(
        functools.partial(_route_kernel, n=n, group=group, cap=cap),
        grid=(ng,),
        in_specs=[pl.BlockSpec((1, N_EXPERTS, t), lambda g: (g, 0, 0))],
        out_specs=pl.BlockSpec((1, N_EXPERTS, t), lambda g: (g, 0, 0)),
        out_shape=jax.ShapeDtypeStruct((ng, N_EXPERTS, t), I32),
        compiler_params=_cparams(("parallel",)),
        name="ec_route",
    )(aff)


def _gather_kernel(slot_ref, aff_ref, h_ref, xe_ref, gate_ref, *, slots):
    t = h_ref.shape[1]
    ne = slot_ref.shape[2]
    sio = lax.broadcasted_iota(I32, (slots, t), 0)
    pm, gates = [], []
    for j in range(ne):
        hit = sio == slot_ref[0, 0, j:j + 1, :]
        pm.append(jnp.where(hit, 1.0, 0.0).astype(BF16))
        gates.append(jnp.sum(jnp.where(hit, aff_ref[0, 0, j:j + 1, :], 0.0), axis=-1, keepdims=True))
    xe = _dot(jnp.concatenate(pm, axis=0), h_ref[0])
    for j in range(ne):
        xe_ref[0, j] = xe[j * slots:(j + 1) * slots].astype(BF16)
        gate_ref[0, j] = jnp.broadcast_to(gates[j], (slots, LANES))


def _ffn_kernel(xe_ref, gate_ref, wg_ref, wu_ref, wd_ref, ye_ref):
    nb, _, s, d = xe_ref.shape
    xe = xe_ref[...].reshape(nb * s, d)
    gate = gate_ref[...].reshape(nb * s, LANES)[:, 0:1]
    hid = (_silu(_dot(xe, wg_ref[0, 0].astype(BF16))) * _dot(xe, wu_ref[0, 0].astype(BF16))).astype(BF16)
    ye = (_dot(hid, wd_ref[0, 0].astype(BF16)) * gate).astype(BF16)
    ye_ref[...] = ye.reshape(nb, 1, s, d)


def _scatter_kernel(slot_ref, ye_ref, x_ref, mod_ref, o_ref, *, slots, chunk):
    tt = x_ref.shape[1]
    ne = slot_ref.shape[1]
    sio = lax.broadcasted_iota(I32, (slots, tt), 0)
    acc = jnp.zeros(x_ref.shape[1:], F32)
    for c0 in range(0, ne, chunk):
        pm = [jnp.where(sio == slot_ref[0, j:j + 1, :], 1.0, 0.0).astype(BF16) for j in range(c0, c0 + chunk)]
        ye = ye_ref[0, c0:c0 + chunk].reshape(chunk * slots, ye_ref.shape[3])
        acc = acc + _dot_tn(jnp.concatenate(pm, axis=0), ye)
    o_ref[0] = x_ref[0] + mod_ref[0, 0, 5:6, :] * acc


def _moe_residual(slot, aff, h2, x_mid, mod, mod_rows, p, layer, group):
    b, n, d = h2.shape
    ng, t = b // group, group * n
    cap = EC_CAPACITY * n // N_EXPERTS
    slots = group * cap
    ech = 4
    xe, gate = pl.pallas_call(
        functools.partial(_gather_kernel, slots=slots),
        grid=(ng, N_EXPERTS // ech),
        in_specs=[
            pl.BlockSpec((1, 1, ech, t), lambda g, c: (g, c, 0, 0)),
            pl.BlockSpec((1, 1, ech, t), lambda g, c: (g, c, 0, 0)),
            pl.BlockSpec((1, t, d), lambda g, c: (g, 0, 0)),
        ],
        out_specs=[pl.BlockSpec((1, ech, slots, d), lambda g, c: (g, c, 0, 0)),
                   pl.BlockSpec((1, ech, slots, LANES), lambda g, c: (g, c, 0, 0))],
        out_shape=[jax.ShapeDtypeStruct((ng, N_EXPERTS, slots, d), BF16),
                   jax.ShapeDtypeStruct((ng, N_EXPERTS, slots, LANES), F32)],
        compiler_params=_cparams(("parallel", "arbitrary")),
        name="ec_gather",
    )(slot.reshape(ng, N_EXPERTS // ech, ech, t), aff.reshape(ng, N_EXPERTS // ech, ech, t),
      h2.reshape(ng, t, d))

    nb = min(ng, 2)
    wsel = lambda e, g: (layer, e, 0, 0)
    ye = pl.pallas_call(
        _ffn_kernel,
        grid=(N_EXPERTS, ng // nb),
        in_specs=[
            pl.BlockSpec((nb, 1, slots, d), lambda e, g: (g, e, 0, 0)),
            pl.BlockSpec((nb, 1, slots, LANES), lambda e, g: (g, e, 0, 0)),
            pl.BlockSpec((1, 1, d, D_EXPERT), wsel),
            pl.BlockSpec((1, 1, d, D_EXPERT), wsel),
            pl.BlockSpec((1, 1, D_EXPERT, d), wsel),
        ],
        out_specs=pl.BlockSpec((nb, 1, slots, d), lambda e, g: (g, e, 0, 0)),
        out_shape=jax.ShapeDtypeStruct((ng, N_EXPERTS, slots, d), BF16),
        compiler_params=_cparams(("parallel", "arbitrary")),
        name="ec_ffn",
    )(xe, gate, p["w_gate"], p["w_up"], p["w_down"])

    tt = min(t, 512)
    nt = t // tt
    if mod_rows == 1:
        mod_map = lambda g, i: (layer, mod.shape[1] - 1, 0, 0)
    else:
        mod_map = lambda g, i: (layer, g, 0, 0)
    out = pl.pallas_call(
        functools.partial(_scatter_kernel, slots=slots, chunk=4),
        grid=(ng, nt),
        in_specs=[
            pl.BlockSpec((1, N_EXPERTS, tt), lambda g, i: (g, 0, i)),
            pl.BlockSpec((1, N_EXPERTS, slots, d), lambda g, i: (g, 0, 0, 0)),
            pl.BlockSpec((1, tt, d), lambda g, i: (g, i, 0)),
            pl.BlockSpec((1, 1, N_MOD, d), mod_map),
        ],
        out_specs=pl.BlockSpec((1, tt, d), lambda g, i: (g, i, 0)),
        out_shape=jax.ShapeDtypeStruct((ng, t, d), F32),
        compiler_params=_cparams(("parallel", "arbitrary")),
        name="ec_scatter",
    )(slot, ye, x_mid.reshape(ng, t, d), mod)
    return out.reshape(b, n, d)


def _rope_tables(n):
    rows = n // GRID_W
    pos = np.stack([np.repeat(np.arange(rows), GRID_W), np.tile(np.arange(GRID_W), rows)], axis=-1)
    n_freq = HEAD_DIM // 4
    inv = jnp.asarray(ROPE_BASE, F32) ** (-jnp.arange(n_freq, dtype=F32) / n_freq)
    ang = jnp.asarray(pos, F32)[..., None] * inv
    cos, sin = jnp.cos(ang), jnp.sin(ang)
    zero = jnp.zeros_like(sin)
    cos64 = jnp.concatenate([cos, cos], axis=-1).reshape(n, HEAD_DIM)
    sa64 = jnp.concatenate([-sin, zero], axis=-1).reshape(n, HEAD_DIM)
    sb64 = jnp.concatenate([zero, sin], axis=-1).reshape(n, HEAD_DIM)
    two = lambda t: jnp.concatenate([t, t], axis=-1)
    return two(cos64), two(sa64), two(sb64)


def _prepare(w_in, norm1_w, norm2_w, q_norm_w, k_norm_w, lambda_q1, lambda_k1, lambda_q2, lambda_k2, subln_w,
             conv_w, conv_b, dt_bias_f, dt_bias_b, a_log_f, a_log_b, d_skip, ssd_norm_w, w_out, w_router,
             w_gate, w_up, w_down):
    nl = w_in.shape[0]
    gi = np.arange(MXU_WIDTH) // HEAD_DIM
    dt_bias = jnp.concatenate([dt_bias_f, dt_bias_b], axis=-1)
    a_log = jnp.concatenate([a_log_f, a_log_b], axis=-1)
    tile2 = lambda w: jnp.concatenate([w, w], axis=-1).reshape(nl, 1, LANES)
    qk_max = jnp.max(jnp.abs(q_norm_w), axis=-1) * jnp.max(jnp.abs(k_norm_w), axis=-1)
    kk = np.arange(CHUNK)[:, None]
    jj = np.arange(SSD_HEADS * CHUNK + SSD_INNER)[None, :]
    to_cs = (kk < 3 * SSD_HEADS) & (jj < SSD_HEADS * CHUNK) & (kk % SSD_HEADS == jj // CHUNK)
    to_ecs = ((kk >= 3 * SSD_HEADS) & (kk < 6 * SSD_HEADS) & (jj >= SSD_HEADS * CHUNK)
              & (kk % SSD_HEADS == (jj - SSD_HEADS * CHUNK) // SSD_HEADDIM))
    return {
        "w_main": w_in[:, :, :P_MAIN].astype(BF16),
        "w_dtt": jnp.swapaxes(w_in[:, :, P_MAIN:], 1, 2).astype(BF16),
        "gmat": jnp.asarray(gi[:, None] == gi[None, :], BF16),
        "norm1_w": norm1_w.reshape(nl, 1, D_MODEL),
        "norm2_w": norm2_w.reshape(nl, 1, D_MODEL),
        "q_norm_w": tile2(q_norm_w),
        "k_norm_w": tile2(k_norm_w),
        "lam": jnp.stack([lambda_q1, lambda_k1, lambda_q2, lambda_k2], axis=1),
        "subln_w": subln_w.reshape(nl, 1, ATTN_VDIM),
        "logit_bound": LOGIT_MARGIN * HEAD_DIM ** 0.5 * qk_max,
        "conv_w": conv_w,
        "conv_b": conv_b.reshape(nl, 1, CONV_DIM),
        "dt_bias_t": dt_bias.reshape(nl, N_DT, 1),
        "a_log_t": a_log.reshape(nl, N_DT, 1),
        "d_skip_x": jnp.repeat(d_skip, SSD_HEADDIM, axis=-1).reshape(nl, 1, SSD_INNER),
        "ssd_norm_w": ssd_norm_w.reshape(nl, 1, SSD_INNER),
        "spread": jnp.asarray(to_cs | to_ecs, BF16),
        "w_out_a": w_out[:, :ATTN_WIDTH, :].astype(BF16),
        "w_out_s": w_out[:, ATTN_WIDTH:, :].astype(BF16),
        "w_router": jnp.pad(w_router, ((0, 0), (0, 0), (0, LANES - N_EXPERTS))).astype(BF16),
        "w_gate": w_gate,
        "w_up": w_up,
        "w_down": w_down,
    }


def _ffn(h2, aff, x_mid, mod, mod_rows, layer, p, group):
    n = h2.shape[1]
    slot = _route(aff, n, group, EC_CAPACITY * n // N_EXPERTS)
    return _moe_residual(slot, aff, h2, x_mid, mod, mod_rows, p, layer, group)


def kernel(x, c, ctx, c_ctx, w_mod, b_mod, norm1_w, norm2_w, w_in, q_norm_w, k_norm_w, lambda_q1, lambda_k1,
           lambda_q2, lambda_k2, subln_w, conv_w, conv_b, dt_bias_f, dt_bias_b, a_log_f, a_log_b, d_skip,
           ssd_norm_w, w_out, w_router, w_gate, w_up, w_down):
    b, n, d = x.shape
    m = ctx.shape[1]
    nl = w_in.shape[0]
    assert d == D_MODEL and n % GRID_W == 0 and n % (2 * CHUNK) == 0 and m % (2 * CHUNK) == 0 and n % m == 0
    assert MOE_TOKENS % n == 0 and MOE_TOKENS % m == 0
    g_lat, g_ctx = MOE_TOKENS // n, MOE_TOKENS // m
    assert b % g_lat == 0 and b % g_ctx == 0
    tm_lat, tm_ctx = min(n, 512), min(m, 256)
    tq = min(n, 512)

    p = _prepare(w_in, norm1_w, norm2_w, q_norm_w, k_norm_w, lambda_q1, lambda_k1, lambda_q2, lambda_k2,
                 subln_w, conv_w, conv_b, dt_bias_f, dt_bias_b, a_log_f, a_log_b, d_skip, ssd_norm_w, w_out,
                 w_router, w_gate, w_up, w_down)
    rope = _rope_tables(n)
    mod = _modulation(jnp.concatenate([c, c_ctx[None, :]], axis=0), w_mod, b_mod)
    h_zero = jnp.zeros((b, 2, SSD_GROUPS, D_STATE, 256), F32)

    x_lat, x_ctx = x, ctx
    for l in range(nl):
        update_ctx = l < nl - 1
        lam_init = 0.8 - 0.6 * math.exp(-0.3 * l)
        q, k_all, v_all, z, xbc, dtt = _in_projection(x_lat, mod, b, l, p, tm_lat, n + m, rope_tabs=rope)
        qc, k_all, v_all, zc, xbcc, dttc = _in_projection(x_ctx, mod, 1, l, p, tm_ctx, n + m, kv=(k_all, v_all))
        ssd_c, h_ctx = _ssd_mixer(xbcc, zc, dttc, h_zero, p, l)
        ssd, _ = _ssd_mixer(xbc, z, dtt, h_ctx, p, l)
        attn = _diff_attention(q, k_all, v_all, p, l, lam_init, tq, n + m, 0)
        x_mid, h2, aff = _out_projection(attn, ssd, x_lat, mod, b, l, p, min(n, 1024), g_lat)
        x_lat = _ffn(h2, aff, x_mid, mod, b, l, p, g_lat)
        if update_ctx:
            attn_c = _diff_attention(qc, k_all, v_all, p, l, lam_init, min(m, 256), m, n // m)
            xc_mid, hc2, aff_c = _out_projection(attn_c, ssd_c, x_ctx, mod, 1, l, p, tm_ctx, g_ctx)
            x_ctx = _ffn(hc2, aff_c, xc_mid, mod, 1, l, p, g_ctx)
    return x_lat
```

```python
import functools
import math

import jax
import jax.numpy as jnp
import numpy as np
from jax import lax
from jax.experimental import pallas as pl
from jax.experimental.pallas import tpu as pltpu

F32 = jnp.float32
BF16 = jnp.bfloat16
I32 = jnp.int32

D_MODEL = 1024
DEPTH = 4
GRID_W = 64
N_MOD = 6
EPS = 1e-6
ATTN_HEADS = 4
HEAD_DIM = 64
ATTN_VDIM = 128
Q_WIDTH = 512
ATTN_WIDTH = 512
ROPE_BASE = 10000.0
SSD_HEADDIM = 64
SSD_GROUPS = 2
SSD_HPG = 4
SSD_HEADS = 8
SSD_INNER = 512
D_STATE = 128
CHUNK = 128
CONV_DIM = 1024
P_MAIN = 3072
N_DT = 2 * SSD_HEADS
N_EXPERTS = 16
EC_CAPACITY = 2
D_EXPERT = 1408
LANES = 128
MXU_WIDTH = 256
MOE_TOKENS = 2048
QK_SCALE = HEAD_DIM ** -0.5 * math.log2(math.e)
LOGIT_MARGIN = 1.03
ATTN_SAFE_LOGIT = 30.0
VMEM_LIMIT = 56 * 1024 * 1024


def _cparams(sem):
    return pltpu.CompilerParams(dimension_semantics=sem, vmem_limit_bytes=VMEM_LIMIT)


def _sigmoid(x):
    return 0.5 * jnp.tanh(0.5 * x) + 0.5


def _silu(x):
    return x * _sigmoid(x)


def _softplus(x):
    return jnp.maximum(x, 0.0) + jnp.log1p(jnp.exp(-jnp.abs(x)))


def _dot(a, b):
    return jnp.dot(a, b, preferred_element_type=F32)


def _dot_nt(a, b):
    return lax.dot_general(a, b, (((1,), (1,)), ((), ())), preferred_element_type=F32)


def _dot_tn(a, b):
    return lax.dot_general(a, b, (((0,), (0,)), ((), ())), preferred_element_type=F32)


def _split3(x):
    hi = x.astype(BF16)
    r = x - hi.astype(F32)
    mid = r.astype(BF16)
    lo = (r - mid.astype(F32)).astype(BF16)
    return hi, mid, lo


def _mod_kernel(cc_ref, w_ref, b_ref, o_ref):
    s = _silu(cc_ref[...]).astype(BF16)
    o_ref[0] = _dot(s, w_ref[0].astype(BF16)) + b_ref[0]


def _modulation(cc, w_mod, b_mod):
    nl, d, nm = w_mod.shape
    r = cc.shape[0]
    tn = 1536
    out = pl.pallas_call(
        _mod_kernel,
        grid=(nl, nm // tn),
        in_specs=[
            pl.BlockSpec((r, d), lambda l, j: (0, 0)),
            pl.BlockSpec((1, d, tn), lambda l, j: (l, 0, j)),
            pl.BlockSpec((1, 1, tn), lambda l, j: (l, 0, j)),
        ],
        out_specs=pl.BlockSpec((1, r, tn), lambda l, j: (l, 0, j)),
        out_shape=jax.ShapeDtypeStruct((nl, r, nm), F32),
        compiler_params=_cparams(("arbitrary", "arbitrary")),
        name="modulation",
    )(cc, w_mod, b_mod.reshape(nl, 1, nm))
    return out.reshape(nl, r, N_MOD, d)


def _rms_rows(x):
    return x * lax.rsqrt(jnp.mean(x * x, axis=-1, keepdims=True) + EPS)


def _head_norm(t, gmat, w128, rope_tabs, scale):
    t2 = (t * t).astype(BF16)
    gw = gmat.shape[0]
    ss = jnp.concatenate([_dot(t2[:, j:j + gw], gmat) for j in range(0, Q_WIDTH, gw)], axis=1)
    y = t * lax.rsqrt(ss * (1.0 / HEAD_DIM) + EPS)
    outs = []
    for j in range(Q_WIDTH // LANES):
        yj = y[:, j * LANES:(j + 1) * LANES] * w128
        if rope_tabs is not None:
            cos, sa, sb = rope_tabs
            up = pltpu.roll(yj, LANES - 16, 1)
            dn = pltpu.roll(yj, 16, 1)
            yj = yj * cos + up * sa + dn * sb
        outs.append(yj * scale)
    return jnp.concatenate(outs, axis=1).astype(BF16)


def _inproj_kernel(*refs, rope):
    if rope:
        (x_ref, mod_ref, nw_ref, w_ref, wdtt_ref, g_ref, qw_ref, kw_ref,
         cos_ref, sa_ref, sb_ref, q_o, k_o, v_o, z_o, xbc_o, dtt_o) = refs
        tabs = (cos_ref[...], sa_ref[...], sb_ref[...])
    else:
        (x_ref, mod_ref, nw_ref, w_ref, wdtt_ref, g_ref, qw_ref, kw_ref, _, _,
         q_o, k_o, v_o, z_o, xbc_o, dtt_o) = refs
        tabs = None
    x = x_ref[0]
    sh = mod_ref[0, 0, 0:1, :]
    sc = mod_ref[0, 0, 1:2, :]
    h = (_rms_rows(x) * nw_ref[0] * (1.0 + sc) + sh).astype(BF16)
    gmat = g_ref[...]
    pq = _dot(h, w_ref[0, :, 0:512])
    pk = _dot(h, w_ref[0, :, 512:1024])
    q_o[0] = _head_norm(pq, gmat, qw_ref[0], tabs, QK_SCALE)
    pv = _dot(h, w_ref[0, :, 1024:1536])
    k_o[0] = _head_norm(pk, gmat, kw_ref[0], tabs, 1.0)
    pz = _dot(h, w_ref[0, :, 1536:2048])
    v_o[0] = pv.astype(BF16)
    px = _dot(h, w_ref[0, :, 2048:3072])
    z_o[0] = pz.astype(BF16)
    dtt_o[0] = _dot_nt(wdtt_ref[0], h)
    xbc_o[0] = px.astype(BF16)


def _in_projection(x, mod, mod_rows, layer, p, tm, kv_rows, rope_tabs=None, kv=None):
    b, n, d = x.shape
    rope = rope_tabs is not None
    assert rope != (kv is not None)
    kv_blk0 = 0 if rope else (kv_rows - n) // tm
    if mod_rows == 1:
        mod_map = lambda bi, i: (layer, mod.shape[1] - 1, 0, 0)
    else:
        mod_map = lambda bi, i: (layer, bi, 0, 0)
    lsel = lambda bi, i: (layer, 0, 0)
    in_specs = [
        pl.BlockSpec((1, tm, d), lambda bi, i: (bi, i, 0)),
        pl.BlockSpec((1, 1, N_MOD, d), mod_map),
        pl.BlockSpec((1, 1, d), lsel),
        pl.BlockSpec((1, d, P_MAIN), lsel),
        pl.BlockSpec((1, N_DT, d), lsel),
        pl.BlockSpec((MXU_WIDTH, MXU_WIDTH), lambda bi, i: (0, 0)),
        pl.BlockSpec((1, 1, LANES), lsel),
        pl.BlockSpec((1, 1, LANES), lsel),
    ]
    args = [x, mod, p["norm1_w"], p["w_main"], p["w_dtt"], p["gmat"], p["q_norm_w"], p["k_norm_w"]]
    if rope:
        in_specs += [pl.BlockSpec((tm, LANES), lambda bi, i: (i, 0))] * 3
        args += list(rope_tabs)
        aliases = {}
    else:
        in_specs += [pl.BlockSpec(memory_space=pl.ANY)] * 2
        aliases = {len(args): 1, len(args) + 1: 2}
        args += list(kv)
    row = lambda w: pl.BlockSpec((1, tm, w), lambda bi, i: (bi, i, 0))
    kv_row = pl.BlockSpec((1, tm, 512), lambda bi, i: (bi, kv_blk0 + i, 0))
    out_specs = [row(512), kv_row, kv_row, row(512), row(1024),
                 pl.BlockSpec((1, N_DT, tm), lambda bi, i: (bi, 0, i))]
    out_shape = [jax.ShapeDtypeStruct((b, n, 512), BF16),
                 jax.ShapeDtypeStruct((b, kv_rows, 512), BF16),
                 jax.ShapeDtypeStruct((b, kv_rows, 512), BF16),
                 jax.ShapeDtypeStruct((b, n, 512), BF16),
                 jax.ShapeDtypeStruct((b, n, 1024), BF16),
                 jax.ShapeDtypeStruct((b, N_DT, n), F32)]
    return pl.pallas_call(
        functools.partial(_inproj_kernel, rope=rope),
        grid=(b, n // tm),
        in_specs=in_specs,
        out_specs=out_specs,
        out_shape=out_shape,
        input_output_aliases=aliases,
        compiler_params=_cparams(("parallel", "arbitrary")),
        name="in_projection_rope" if rope else "in_projection",
    )(*args)


def _attn_kernel(bound_ref, q_ref, k_ref, v_ref, lam_ref, sw_ref, o_ref, *, layer, lam_init):
    tq = q_ref.shape[1]
    lp = lam_ref[0]
    lam = (jnp.exp(jnp.sum(lp[0:1] * lp[1:2], axis=-1, keepdims=True))
           - jnp.exp(jnp.sum(lp[2:3] * lp[3:4], axis=-1, keepdims=True)) + lam_init)
    lane = lax.broadcasted_iota(I32, (1, LANES), 1)
    m_lo = jnp.where(lane < HEAD_DIM, 1.0, 0.0).astype(BF16)
    m_hi = jnp.where(lane < HEAD_DIM, 0.0, 1.0).astype(BF16)
    sw = sw_ref[0]

    def heads(shift_by_max):
        for h in range(ATTN_HEADS):
            cols = slice(h * LANES, (h + 1) * LANES)
            qh = q_ref[0, :, cols]
            qs = jnp.concatenate([qh * m_lo, qh * m_hi], axis=0)
            s = _dot_nt(qs, k_ref[0, :, cols])
            if shift_by_max:
                s = s - jnp.max(s, axis=-1, keepdims=True)
            p = jnp.exp2(s)
            r = 1.0 / jnp.sum(p, axis=-1, keepdims=True)
            pv = _dot(p.astype(BF16), v_ref[0, :, cols]) * r
            o = pv[0:tq] - lam * pv[tq:2 * tq]
            o = _rms_rows(o) * sw * (1.0 - lam_init)
            o_ref[0, :, cols] = o.astype(BF16)

    small = bound_ref[layer] < ATTN_SAFE_LOGIT

    @pl.when(small)
    def _():
        heads(False)

    @pl.when(jnp.logical_not(small))
    def _():
        heads(True)


def _diff_attention(q, k_all, v_all, p, layer, lam_init, tq, nk, kv_block):
    b, n, _ = q.shape
    return pl.pallas_call(
        functools.partial(_attn_kernel, layer=layer, lam_init=lam_init),
        grid=(b, n // tq),
        in_specs=[
            pl.BlockSpec(memory_space=pltpu.SMEM),
            pl.BlockSpec((1, tq, Q_WIDTH), lambda bi, i: (bi, i, 0)),
            pl.BlockSpec((1, nk, Q_WIDTH), lambda bi, i: (bi, kv_block, 0)),
            pl.BlockSpec((1, nk, ATTN_WIDTH), lambda bi, i: (bi, kv_block, 0)),
            pl.BlockSpec((1, 4, HEAD_DIM), lambda bi, i: (layer, 0, 0)),
            pl.BlockSpec((1, 1, ATTN_VDIM), lambda bi, i: (layer, 0, 0)),
        ],
        out_specs=pl.BlockSpec((1, tq, ATTN_WIDTH), lambda bi, i: (bi, i, 0)),
        out_shape=jax.ShapeDtypeStruct((b, n, ATTN_WIDTH), BF16),
        compiler_params=_cparams(("parallel", "arbitrary")),
        name="diff_attention",
    )(p["logit_bound"], q, k_all, v_all, p["lam"], p["subln_w"])


def _ssd_kernel(xbc_ref, z_ref, dtt_ref, cw_ref, cb_ref, dtbt_ref, alt_ref, dsk_ref, nw_ref, sp_ref, h0_ref,
                y_ref, hfin_ref, xs_s, c_s, bt_s, yf_s, yb_s, hf0_s, hf1_s, hb0_s, hb1_s, *, n):
    nc = n // CHUNK
    ht_s = ((hf0_s, hf1_s), (hb0_s, hb1_s))
    yacc_s = (yf_s, yb_s)
    row = lax.broadcasted_iota(I32, (CHUNK, CHUNK), 0)
    col = lax.broadcasted_iota(I32, (CHUNK, CHUNK), 1)
    tri_lo = col <= row
    tri_up = col >= row
    t_lo = jnp.where(tri_lo, 1.0, 0.0).astype(BF16)
    t_up = jnp.where(tri_up, 1.0, 0.0).astype(BF16)
    a_col = -jnp.exp(alt_ref[0])
    lane_head = lax.broadcasted_iota(I32, (1, SSD_INNER // 2), 1) // SSD_HEADDIM
    head_masks_f = [jnp.where(lane_head == r, 1.0, 0.0) for r in range(SSD_HPG)]
    head_masks = [m.astype(BF16) for m in head_masks_f]
    zero_rows = jnp.zeros((CHUNK - 6 * SSD_HEADS, CHUNK), F32)
    spread_mat = sp_ref[...]


    def scan_sums(c, d):
        r0 = pl.multiple_of(c * CHUNK, CHUNK)
        hs = slice(d * SSD_HEADS, (d + 1) * SSD_HEADS)
        dt_r = _softplus(dtt_ref[0, hs, pl.ds(r0, CHUNK)] + dtbt_ref[0, hs, :])
        hi, mid, lo = _split3(dt_r * a_col[hs])
        da3 = jnp.concatenate([hi.astype(F32), mid.astype(F32), lo.astype(F32)], axis=0).astype(BF16)
        cs3 = _dot(da3, t_up if d == 0 else t_lo)
        cs_r = cs3[0:8] + cs3[8:16] + cs3[16:24]
        edge = CHUNK - 1 if d == 0 else 0
        cs_edge = cs_r[:, edge:edge + 1]
        return dict(c=c, d=d, r0=r0, dt_r=dt_r, cs_r=cs_r,
                    wst_r=(dt_r * jnp.exp(cs_edge - cs_r)).astype(BF16),
                    cdec=jnp.exp(cs_edge))

    def scan_spread(st):
        cs_r, r0 = st["cs_r"], st["r0"]
        pieces = [t.astype(F32) for t in _split3(cs_r) + _split3(jnp.exp(cs_r))]
        colf = jnp.concatenate(pieces + [zero_rows], axis=0).T.astype(BF16)
        st["spread"] = _dot(colf, spread_mat)
        st["xs"] = xs_s[pl.ds(r0, CHUNK), :]
        st["xs_b"] = st["xs"].astype(BF16)
        st["cm"] = c_s[pl.ds(r0, CHUNK), :]
        st["bt"] = bt_s[:, pl.ds(r0, CHUNK)]
        st["ys"] = []

    def scan_group(st, g):
        d, cs_r, dt_r, spread = st["d"], st["cs_r"], st["dt_r"], st["spread"]
        mask = tri_lo if d == 0 else tri_up
        gl = slice(g * 256, (g + 1) * 256)
        cg = st["cm"][:, g * D_STATE:(g + 1) * D_STATE]
        btg = st["bt"][g * D_STATE:(g + 1) * D_STATE, :]
        cbg = _dot(cg, btg)
        htg = ht_s[d][g][...]
        y_off = _dot(cg, htg.astype(BF16)) * spread[:, SSD_HEADS * CHUNK + g * 256:SSD_HEADS * CHUNK + (g + 1) * 256]
        lms, bts, rstack = [], [], []
        cdec_x = jnp.zeros((1, 256), F32)
        for r in range(SSD_HPG):
            hh = g * SSD_HPG + r
            seg = spread[:, hh * CHUNK:(hh + 1) * CHUNK] - cs_r[hh:hh + 1, :]
            dec = jnp.exp(jnp.where(mask, seg, -jnp.inf))
            lms.append((cbg * dec * dt_r[hh:hh + 1, :]).astype(BF16))
            bts.append(btg * st["wst_r"][hh:hh + 1, :])
            rstack.append(st["xs_b"][:, gl] * head_masks[r])
            cdec_x = cdec_x + head_masks_f[r] * st["cdec"][hh:hh + 1, :]
        lhs = jnp.concatenate([jnp.concatenate(lms, axis=1), jnp.concatenate(bts, axis=1)], axis=0)
        res = _dot(lhs, jnp.concatenate(rstack, axis=0))
        st["ys"].append(res[0:CHUNK] + y_off)
        ht_s[d][g][...] = htg * cdec_x + res[CHUNK:2 * CHUNK]

    def scan_pairs(k0, count):
        sts = [scan_sums(c, d) for j in range(count) for c, d in ((k0 + j, 0), (nc - 1 - k0 - j, 1))]
        for st in sts:
            scan_spread(st)
        for j in range(count):
            for g in range(SSD_GROUPS):
                for st in sts[2 * j:2 * j + 2]:
                    scan_group(st, g)
        for st in sts:
            st["y"] = jnp.concatenate(st["ys"], axis=1)
        return sts

    for d in range(2):
        for g in range(SSD_GROUPS):
            ht_s[d][g][...] = h0_ref[0, d, g]
    cw = cw_ref[0]
    rows1 = lax.broadcasted_iota(I32, (CHUNK, 1), 0)

    def conv_body(c, carry):
        r0 = pl.multiple_of(c * CHUNK, CHUNK)
        x = xbc_ref[0, pl.ds(r0, CHUNK), :].astype(F32)
        rp0 = pl.multiple_of(jnp.maximum(r0 - 16, 0), 16)
        rn0 = pl.multiple_of(jnp.minimum(r0 + CHUNK, n - 16), 16)
        prev = xbc_ref[0, pl.ds(rp0, 16), :].astype(F32)[15:16, :]
        nxt = xbc_ref[0, pl.ds(rn0, 16), :].astype(F32)[0:1, :]
        prev = jnp.where(c > 0, prev, 0.0)
        nxt = jnp.where(c < nc - 1, nxt, 0.0)
        x_prev = jnp.where(rows1 == 0, prev, pltpu.roll(x, 1, 0))
        x_next = jnp.where(rows1 == CHUNK - 1, nxt, pltpu.roll(x, CHUNK - 1, 0))
        xc = _silu(cb_ref[0] + x_prev * cw[0:1] + x * cw[1:2] + x_next * cw[2:3])
        bm = xc[:, SSD_INNER:SSD_INNER + 256]
        bt = jnp.concatenate([bm[:, 0:D_STATE].T, bm[:, D_STATE:2 * D_STATE].T], axis=0).astype(BF16)
        xs_s[pl.ds(r0, CHUNK), :] = xc[:, 0:SSD_INNER]
        c_s[pl.ds(r0, CHUNK), :] = xc[:, SSD_INNER + 256:SSD_INNER + 512].astype(BF16)
        bt_s[:, pl.ds(r0, CHUNK)] = bt
        return carry

    lax.fori_loop(0, nc, conv_body, 0)

    def finish_chunk(st):
        r0, d = st["r0"], st["d"]
        y = yacc_s[1 - d][pl.ds(r0, CHUNK), :] + st["y"] + dsk_ref[0] * st["xs"]
        zz = z_ref[0, pl.ds(r0, CHUNK), :].astype(F32)
        g = y * _silu(zz)
        outs = [_rms_rows(g[:, gi * 256:(gi + 1) * 256]) for gi in range(SSD_GROUPS)]
        y_ref[0, pl.ds(r0, CHUNK), :] = (jnp.concatenate(outs, axis=1) * nw_ref[0]).astype(BF16)

    half = nc // 2
    per_step = math.gcd(half, 4)

    def outer_body(i, carry):
        for st in scan_pairs(i * per_step, per_step):
            yacc_s[st["d"]][pl.ds(st["r0"], CHUNK), :] = st["y"]
        return carry

    def inner_body(i, carry):
        for st in scan_pairs(half + i * per_step, per_step):
            finish_chunk(st)
        return carry

    lax.fori_loop(0, half // per_step, outer_body, 0)
    lax.fori_loop(0, half // per_step, inner_body, 0)
    for d in range(2):
        for g in range(SSD_GROUPS):
            hfin_ref[0, d, g] = ht_s[d][g][...]


def _ssd_mixer(xbc, z, dtt, h0, p, layer):
    b, n, _ = xbc.shape
    lsel3 = lambda bi: (layer, 0, 0)
    st_spec = pl.BlockSpec((1, 2, SSD_GROUPS, D_STATE, 256), lambda bi: (bi, 0, 0, 0, 0))
    return pl.pallas_call(
        functools.partial(_ssd_kernel, n=n),
        grid=(b,),
        in_specs=[
            pl.BlockSpec((1, n, CONV_DIM), lambda bi: (bi, 0, 0)),
            pl.BlockSpec((1, n, SSD_INNER), lambda bi: (bi, 0, 0)),
            pl.BlockSpec((1, N_DT, n), lambda bi: (bi, 0, 0)),
            pl.BlockSpec((1, 3, CONV_DIM), lsel3),
            pl.BlockSpec((1, 1, CONV_DIM), lsel3),
            pl.BlockSpec((1, N_DT, 1), lsel3),
            pl.BlockSpec((1, N_DT, 1), lsel3),
            pl.BlockSpec((1, 1, SSD_INNER), lsel3),
            pl.BlockSpec((1, 1, SSD_INNER), lsel3),
            pl.BlockSpec((CHUNK, SSD_HEADS * CHUNK + SSD_INNER), lambda bi: (0, 0)),
            st_spec,
        ],
        out_specs=[pl.BlockSpec((1, n, SSD_INNER), lambda bi: (bi, 0, 0)), st_spec],
        out_shape=[jax.ShapeDtypeStruct((b, n, SSD_INNER), BF16),
                   jax.ShapeDtypeStruct((b, 2, SSD_GROUPS, D_STATE, 256), F32)],
        scratch_shapes=[
            pltpu.VMEM((n, SSD_INNER), F32),
            pltpu.VMEM((n, 2 * D_STATE), BF16),
            pltpu.VMEM((2 * D_STATE, n), BF16),
            pltpu.VMEM((n, SSD_INNER), F32),
            pltpu.VMEM((n, SSD_INNER), F32),
        ] + [pltpu.VMEM((D_STATE, 256), F32)] * (2 * SSD_GROUPS),
        compiler_params=_cparams(("parallel",)),
        name="ssd_mixer",
    )(xbc, z, dtt, p["conv_w"], p["conv_b"], p["dt_bias_t"], p["a_log_t"], p["d_skip_x"], p["ssd_norm_w"],
      p["spread"], h0)


def _outproj_kernel(a_ref, s_ref, x_ref, mod_ref, wa_ref, ws_ref, nw_ref, wr_ref, xo_ref, h2_ref, aff_ref):
    tm = x_ref.shape[1]
    half = tm // 2 if tm >= 512 else tm
    blocks = [slice(r, r + half) for r in range(0, tm, half)]
    mixes = [_dot(a_ref[0, rows, :], wa_ref[0]) + _dot(s_ref[0, rows, :], ws_ref[0]) for rows in blocks]
    for rows, mix in zip(blocks, mixes):
        x = x_ref[0, rows, :] + mod_ref[0, 0, 2:3, :] * mix
        xo_ref[0, rows, :] = x
        h2 = (_rms_rows(x) * nw_ref[0] * (1.0 + mod_ref[0, 0, 4:5, :]) + mod_ref[0, 0, 3:4, :]).astype(BF16)
        h2_ref[0, rows, :] = h2
        lg = _dot(h2, wr_ref[0])
        lgt = lg.T[0:N_EXPERTS, :]
        e = jnp.exp(lgt - jnp.max(lgt, axis=0, keepdims=True))
        aff_ref[0, :, rows] = e / jnp.sum(e, axis=0, keepdims=True)


def _out_projection(attn, ssd, x, mod, mod_rows, layer, p, tm, group):
    b, n, d = x.shape
    nt = n // tm
    if mod_rows == 1:
        mod_map = lambda bi, i: (layer, mod.shape[1] - 1, 0, 0)
    else:
        mod_map = lambda bi, i: (layer, bi, 0, 0)
    lsel = lambda bi, i: (layer, 0, 0)
    row = lambda w: pl.BlockSpec((1, tm, w), lambda bi, i: (bi, i, 0))
    return pl.pallas_call(
        _outproj_kernel,
        grid=(b, nt),
        in_specs=[row(ATTN_WIDTH), row(SSD_INNER), row(d),
                  pl.BlockSpec((1, 1, N_MOD, d), mod_map),
                  pl.BlockSpec((1, ATTN_WIDTH, d), lsel),
                  pl.BlockSpec((1, SSD_INNER, d), lsel),
                  pl.BlockSpec((1, 1, d), lsel),
                  pl.BlockSpec((1, d, LANES), lsel)],
        out_specs=[row(d), row(d),
                   pl.BlockSpec((1, N_EXPERTS, tm), lambda bi, i: (bi // group, 0, (bi % group) * nt + i))],
        out_shape=[jax.ShapeDtypeStruct((b, n, d), F32),
                   jax.ShapeDtypeStruct((b, n, d), BF16),
                   jax.ShapeDtypeStruct((b // group, N_EXPERTS, group * n), F32)],
        compiler_params=_cparams(("parallel", "arbitrary")),
        name="out_projection",
    )(attn, ssd, x, mod, p["w_out_a"], p["w_out_s"], p["norm2_w"], p["w_router"])


def _route_kernel(aff_ref, slot_ref, *, n, group, cap):
    bits_all = pltpu.bitcast(aff_ref[0], I32)
    li = lax.broadcasted_iota(I32, (LANES, LANES), 0)
    lj = lax.broadcasted_iota(I32, (LANES, LANES), 1)
    strict_up = jnp.where(li < lj, 1.0, 0.0).astype(BF16)
    fcap = float(cap)

    def count(m):
        return jnp.sum(jnp.where(m, 1.0, 0.0), axis=-1, keepdims=True)

    def excl_prefix(m):
        total = jnp.zeros((N_EXPERTS, 1), F32)
        outs = []
        for j in range(n // LANES):
            blk = jnp.where(m[:, j * LANES:(j + 1) * LANES], 1.0, 0.0)
            outs.append(_dot(blk.astype(BF16), strict_up) + total)
            total = total + jnp.sum(blk, axis=-1, keepdims=True)
        return jnp.concatenate(outs, axis=1)

    for s in range(group):
        bits = bits_all[:, s * n:(s + 1) * n]

        def keep(cand, lo):
            return jnp.where(count(bits >= cand) >= fcap, cand, lo)

        def two_bits(i, lo):
            b1 = jnp.int32(1) << (30 - 2 * i)
            b0 = jnp.int32(1) << (29 - 2 * i)
            return keep(lo | b1 | b0, keep(lo | b1, keep(lo | b0, lo)))

        thr = lax.fori_loop(0, 15, two_bits, jnp.zeros((N_EXPERTS, 1), I32))
        thr = keep(thr | 1, thr)
        gt = bits > thr
        eq = bits == thr
        need = fcap - count(gt)
        sel = gt | (eq & (excl_prefix(eq) < need))
        pos = excl_prefix(sel)
        slot_ref[0, :, s * n:(s + 1) * n] = jnp.where(sel, pos.astype(I32) + s * cap, -1)


def _route(aff, n, group, cap):
    ng, _, t = aff.shape
    return pl.pallas_call(
        functools.partial(_route_kernel, n=n, group=group, cap=cap),
        grid=(ng,),
        in_specs=[pl.BlockSpec((1, N_EXPERTS, t), lambda g: (g, 0, 0))],
        out_specs=pl.BlockSpec((1, N_EXPERTS, t), lambda g: (g, 0, 0)),
        out_shape=jax.ShapeDtypeStruct((ng, N_EXPERTS, t), I32),
        compiler_params=_cparams(("parallel",)),
        name="ec_route",
    )(aff)


def _gather_kernel(slot_ref, aff_ref, h_ref, xe_ref, gate_ref, *, slots):
    t = h_ref.shape[1]
    ne = slot_ref.shape[2]
    sio = lax.broadcasted_iota(I32, (slots, t), 0)
    pm, gates = [], []
    for j in range(ne):
        hit = sio == slot_ref[0, 0, j:j + 1, :]
        pm.append(jnp.where(hit, 1.0, 0.0).astype(BF16))
        gates.append(jnp.sum(jnp.where(hit, aff_ref[0, 0, j:j + 1, :], 0.0), axis=-1, keepdims=True))
    xe = _dot(jnp.concatenate(pm, axis=0), h_ref[0])
    for j in range(ne):
        xe_ref[0, j] = xe[j * slots:(j + 1) * slots].astype(BF16)
        gate_ref[0, j] = jnp.broadcast_to(gates[j], (slots, LANES))


def _ffn_kernel(xe_ref, gate_ref, wg_ref, wu_ref, wd_ref, ye_ref):
    nb, _, s, d = xe_ref.shape
    xe = xe_ref[...].reshape(nb * s, d)
    gate = gate_ref[...].reshape(nb * s, LANES)[:, 0:1]
    hid = (_silu(_dot(xe, wg_ref[0, 0].astype(BF16))) * _dot(xe, wu_ref[0, 0].astype(BF16))).astype(BF16)
    ye = (_dot(hid, wd_ref[0, 0].astype(BF16)) * gate).astype(BF16)
    ye_ref[...] = ye.reshape(nb, 1, s, d)


def _scatter_kernel(slot_ref, ye_ref, x_ref, mod_ref, o_ref, *, slots, chunk):
    tt = x_ref.shape[1]
    ne = slot_ref.shape[1]
    sio = lax.broadcasted_iota(I32, (slots, tt), 0)
    acc = jnp.zeros(x_ref.shape[1:], F32)
    for c0 in range(0, ne, chunk):
        pm = [jnp.where(sio == slot_ref[0, j:j + 1, :], 1.0, 0.0).astype(BF16) for j in range(c0, c0 + chunk)]
        ye = ye_ref[0, c0:c0 + chunk].reshape(chunk * slots, ye_ref.shape[3])
        acc = acc + _dot_tn(jnp.concatenate(pm, axis=0), ye)
    o_ref[0] = x_ref[0] + mod_ref[0, 0, 5:6, :] * acc


def _moe_residual(slot, aff, h2, x_mid, mod, mod_rows, p, layer, group):
    b, n, d = h2.shape
    ng, t = b // group, group * n
    cap = EC_CAPACITY * n // N_EXPERTS
    slots = group * cap
    ech = 4
    xe, gate = pl.pallas_call(
        functools.partial(_gather_kernel, slots=slots),
        grid=(ng, N_EXPERTS // ech),
        in_specs=[
            pl.BlockSpec((1, 1, ech, t), lambda g, c: (g, c, 0, 0)),
            pl.BlockSpec((1, 1, ech, t), lambda g, c: (g, c, 0, 0)),
            pl.BlockSpec((1, t, d), lambda g, c: (g, 0, 0)),
        ],
        out_specs=[pl.BlockSpec((1, ech, slots, d), lambda g, c: (g, c, 0, 0)),
                   pl.BlockSpec((1, ech, slots, LANES), lambda g, c: (g, c, 0, 0))],
        out_shape=[jax.ShapeDtypeStruct((ng, N_EXPERTS, slots, d), BF16),
                   jax.ShapeDtypeStruct((ng, N_EXPERTS, slots, LANES), F32)],
        compiler_params=_cparams(("parallel", "arbitrary")),
        name="ec_gather",
    )(slot.reshape(ng, N_EXPERTS // ech, ech, t), aff.reshape(ng, N_EXPERTS // ech, ech, t),
      h2.reshape(ng, t, d))

    nb = min(ng, 2)
    wsel = lambda e, g: (layer, e, 0, 0)
    ye = pl.pallas_call(
        _ffn_kernel,
        grid=(N_EXPERTS, ng // nb),
        in_specs=[
            pl.BlockSpec((nb, 1, slots, d), lambda e, g: (g, e, 0, 0)),
            pl.BlockSpec((nb, 1, slots, LANES), lambda e, g: (g, e, 0, 0)),
            pl.BlockSpec((1, 1, d, D_EXPERT), wsel),
            pl.BlockSpec((1, 1, d, D_EXPERT), wsel),
            pl.BlockSpec((1, 1, D_EXPERT, d), wsel),
        ],
        out_specs=pl.BlockSpec((nb, 1, slots, d), lambda e, g: (g, e, 0, 0)),
        out_shape=jax.ShapeDtypeStruct((ng, N_EXPERTS, slots, d), BF16),
        compiler_params=_cparams(("parallel", "arbitrary")),
        name="ec_ffn",
    )(xe, gate, p["w_gate"], p["w_up"], p["w_down"])

    tt = min(t, 512)
    nt = t // tt
    if mod_rows == 1:
        mod_map = lambda g, i: (layer, mod.shape[1] - 1, 0, 0)
    else:
        mod_map = lambda g, i: (layer, g, 0, 0)
    out = pl.pallas_call(
        functools.partial(_scatter_kernel, slots=slots, chunk=4),
        grid=(ng, nt),
        in_specs=[
            pl.BlockSpec((1, N_EXPERTS, tt), lambda g, i: (g, 0, i)),
            pl.BlockSpec((1, N_EXPERTS, slots, d), lambda g, i: (g, 0, 0, 0)),
            pl.BlockSpec((1, tt, d), lambda g, i: (g, i, 0)),
            pl.BlockSpec((1, 1, N_MOD, d), mod_map),
        ],
        out_specs=pl.BlockSpec((1, tt, d), lambda g, i: (g, i, 0)),
        out_shape=jax.ShapeDtypeStruct((ng, t, d), F32),
        compiler_params=_cparams(("parallel", "arbitrary")),
        name="ec_scatter",
    )(slot, ye, x_mid.reshape(ng, t, d), mod)
    return out.reshape(b, n, d)


def _rope_tables(n):
    rows = n // GRID_W
    pos = np.stack([np.repeat(np.arange(rows), GRID_W), np.tile(np.arange(GRID_W), rows)], axis=-1)
    n_freq = HEAD_DIM // 4
    inv = jnp.asarray(ROPE_BASE, F32) ** (-jnp.arange(n_freq, dtype=F32) / n_freq)
    ang = jnp.asarray(pos, F32)[..., None] * inv
    cos, sin = jnp.cos(ang), jnp.sin(ang)
    zero = jnp.zeros_like(sin)
    cos64 = jnp.concatenate([cos, cos], axis=-1).reshape(n, HEAD_DIM)
    sa64 = jnp.concatenate([-sin, zero], axis=-1).reshape(n, HEAD_DIM)
    sb64 = jnp.concatenate([zero, sin], axis=-1).reshape(n, HEAD_DIM)
    two = lambda t: jnp.concatenate([t, t], axis=-1)
    return two(cos64), two(sa64), two(sb64)


def _prepare(w_in, norm1_w, norm2_w, q_norm_w, k_norm_w, lambda_q1, lambda_k1, lambda_q2, lambda_k2, subln_w,
             conv_w, conv_b, dt_bias_f, dt_bias_b, a_log_f, a_log_b, d_skip, ssd_norm_w, w_out, w_router,
             w_gate, w_up, w_down):
    nl = w_in.shape[0]
    gi = np.arange(MXU_WIDTH) // HEAD_DIM
    dt_bias = jnp.concatenate([dt_bias_f, dt_bias_b], axis=-1)
    a_log = jnp.concatenate([a_log_f, a_log_b], axis=-1)
    tile2 = lambda w: jnp.concatenate([w, w], axis=-1).reshape(nl, 1, LANES)
    qk_max = jnp.max(jnp.abs(q_norm_w), axis=-1) * jnp.max(jnp.abs(k_norm_w), axis=-1)
    kk = np.arange(CHUNK)[:, None]
    jj = np.arange(SSD_HEADS * CHUNK + SSD_INNER)[None, :]
    to_cs = (kk < 3 * SSD_HEADS) & (jj < SSD_HEADS * CHUNK) & (kk % SSD_HEADS == jj // CHUNK)
    to_ecs = ((kk >= 3 * SSD_HEADS) & (kk < 6 * SSD_HEADS) & (jj >= SSD_HEADS * CHUNK)
              & (kk % SSD_HEADS == (jj - SSD_HEADS * CHUNK) // SSD_HEADDIM))
    return {
        "w_main": w_in[:, :, :P_MAIN].astype(BF16),
        "w_dtt": jnp.swapaxes(w_in[:, :, P_MAIN:], 1, 2).astype(BF16),
        "gmat": jnp.asarray(gi[:, None] == gi[None, :], BF16),
        "norm1_w": norm1_w.reshape(nl, 1, D_MODEL),
        "norm2_w": norm2_w.reshape(nl, 1, D_MODEL),
        "q_norm_w": tile2(q_norm_w),
        "k_norm_w": tile2(k_norm_w),
        "lam": jnp.stack([lambda_q1, lambda_k1, lambda_q2, lambda_k2], axis=1),
        "subln_w": subln_w.reshape(nl, 1, ATTN_VDIM),
        "logit_bound": LOGIT_MARGIN * HEAD_DIM ** 0.5 * qk_max,
        "conv_w": conv_w,
        "conv_b": conv_b.reshape(nl, 1, CONV_DIM),
        "dt_bias_t": dt_bias.reshape(nl, N_DT, 1),
        "a_log_t": a_log.reshape(nl, N_DT, 1),
        "d_skip_x": jnp.repeat(d_skip, SSD_HEADDIM, axis=-1).reshape(nl, 1, SSD_INNER),
        "ssd_norm_w": ssd_norm_w.reshape(nl, 1, SSD_INNER),
        "spread": jnp.asarray(to_cs | to_ecs, BF16),
        "w_out_a": w_out[:, :ATTN_WIDTH, :].astype(BF16),
        "w_out_s": w_out[:, ATTN_WIDTH:, :].astype(BF16),
        "w_router": jnp.pad(w_router, ((0, 0), (0, 0), (0, LANES - N_EXPERTS))).astype(BF16),
        "w_gate": w_gate,
        "w_up": w_up,
        "w_down": w_down,
    }


def _ffn(h2, aff, x_mid, mod, mod_rows, layer, p, group):
    n = h2.shape[1]
    slot = _route(aff, n, group, EC_CAPACITY * n // N_EXPERTS)
    return _moe_residual(slot, aff, h2, x_mid, mod, mod_rows, p, layer, group)


def kernel(x, c, ctx, c_ctx, w_mod, b_mod, norm1_w, norm2_w, w_in, q_norm_w, k_norm_w, lambda_q1, lambda_k1,
           lambda_q2, lambda_k2, subln_w, conv_w, conv_b, dt_bias_f, dt_bias_b, a_log_f, a_log_b, d_skip,
           ssd_norm_w, w_out, w_router, w_gate, w_up, w_down):
    b, n, d = x.shape
    m = ctx.shape[1]
    nl = w_in.shape[0]
    assert d == D_MODEL and n % GRID_W == 0 and n % (2 * CHUNK) == 0 and m % (2 * CHUNK) == 0 and n % m == 0
    assert MOE_TOKENS % n == 0 and MOE_TOKENS % m == 0
    g_lat, g_ctx = MOE_TOKENS // n, MOE_TOKENS // m
    assert b % g_lat == 0 and b % g_ctx == 0
    tm_lat, tm_ctx = min(n, 512), min(m, 256)
    tq = min(n, 512)

    p = _prepare(w_in, norm1_w, norm2_w, q_norm_w, k_norm_w, lambda_q1, lambda_k1, lambda_q2, lambda_k2,
                 subln_w, conv_w, conv_b, dt_bias_f, dt_bias_b, a_log_f, a_log_b, d_skip, ssd_norm_w, w_out,
                 w_router, w_gate, w_up, w_down)
    rope = _rope_tables(n)
    mod = _modulation(jnp.concatenate([c, c_ctx[None, :]], axis=0), w_mod, b_mod)
    h_zero = jnp.zeros((b, 2, SSD_GROUPS, D_STATE, 256), F32)

    x_lat, x_ctx = x, ctx
    for l in range(nl):
        update_ctx = l < nl - 1
        lam_init = 0.8 - 0.6 * math.exp(-0.3 * l)
        q, k_all, v_all, z, xbc, dtt = _in_projection(x_lat, mod, b, l, p, tm_lat, n + m, rope_tabs=rope)
        qc, k_all, v_all, zc, xbcc, dttc = _in_projection(x_ctx, mod, 1, l, p, tm_ctx, n + m, kv=(k_all, v_all))
        ssd_c, h_ctx = _ssd_mixer(xbcc, zc, dttc, h_zero, p, l)
        ssd, _ = _ssd_mixer(xbc, z, dtt, h_ctx, p, l)
        attn = _diff_attention(q, k_all, v_all, p, l, lam_init, tq, n + m, 0)
        x_mid, h2, aff = _out_projection(attn, ssd, x_lat, mod, b, l, p, min(n, 1024), g_lat)
        x_lat = _ffn(h2, aff, x_mid, mod, b, l, p, g_lat)
        if update_ctx:
            attn_c = _diff_attention(qc, k_all, v_all, p, l, lam_init, min(m, 256), m, n // m)
            xc_mid, hc2, aff_c = _out_projection(attn_c, ssd_c, x_ctx, mod, 1, l, p, tm_ctx, g_ctx)
            x_ctx = _ffn(hc2, aff_c, xc_mid, mod, 1, l, p, g_ctx)
    return x_lat
```
